```python
import jax
import jax.numpy as jnp
from jax import lax
import numpy as np

D_MODEL = 1024
BATCH = 4
SEQ = 8192
DEPTH = 2

GRID_W = 64
CTX_LEN = 256
D_FF = 4 * D_MODEL
NORM_EPS = 1e-6
ROPE_THETA = 10000.0
Q_BLOCK = 128
N_MOD = 6

MLA_HEADS = 8
MLA_Q_RANK = 256
MLA_KV_RANK = 128
MLA_NOPE = 64
MLA_ROPE = 32
MLA_V = 64

ML_HEADS = 4
ML_DK = 128
ML_DV = 128
ML_CONV = 3
ML_CHUNK = 128
ML_F_BIAS_LO = 3.0
ML_F_BIAS_HI = 6.0

GQA_HEADS = 8
GQA_KV_HEADS = 2
GQA_GROUP = GQA_HEADS // GQA_KV_HEADS
GQA_DH = 128

N_GATES = 4 * ML_HEADS
EVEN_WIDTHS = (MLA_Q_RANK, MLA_KV_RANK, MLA_ROPE, 2 * ML_HEADS * ML_DK, ML_HEADS * ML_DV, ML_HEADS * ML_DV, N_GATES)
EVEN_SPLITS = tuple(int(s) for s in np.cumsum(EVEN_WIDTHS)[:-1])
EVEN_IN = sum(EVEN_WIDTHS)
EVEN_MIX = MLA_HEADS * MLA_V + ML_HEADS * ML_DV
ODD_SPLITS = (GQA_HEADS * GQA_DH, (GQA_HEADS + GQA_KV_HEADS) * GQA_DH)
ODD_IN = (GQA_HEADS + 2 * GQA_KV_HEADS) * GQA_DH
ODD_MIX = GQA_HEADS * GQA_DH

kernel_name = 'hybrid_mla_mlstm_gqa_dit_block'


def rms_norm(x, g):
    xf = x.astype(jnp.float32)
    y = xf * lax.rsqrt(jnp.mean(xf * xf, axis=-1, keepdims=True) + NORM_EPS)
    return (y * g.astype(jnp.float32)).astype(x.dtype)


def modulate(x, g, shift, scale):
    return rms_norm(x, g) * (1 + scale[:, None, :]) + shift[:, None, :]


def adaln(cvec, ada_w, ada_b):
    return jnp.split(jax.nn.silu(cvec) @ ada_w + ada_b, N_MOD, axis=-1)


def grid_angles(n_tokens, d_rot):
    rows = n_tokens // GRID_W
    row, col = jnp.meshgrid(jnp.arange(rows), jnp.arange(GRID_W), indexing='ij')
    pos = jnp.stack([row.reshape(-1), col.reshape(-1)], axis=-1).astype(jnp.float32)
    n_freq = d_rot // 4
    freqs = ROPE_THETA ** (-jnp.arange(n_freq, dtype=jnp.float32) / n_freq)
    return pos[:, :, None] * freqs


def apply_rope(x, ang):
    xr = x.reshape(x.shape[:-1] + (2, 2, ang.shape[-1]))
    x0, x1 = xr[..., 0, :], xr[..., 1, :]
    cos = jnp.cos(ang).astype(x.dtype)
    sin = jnp.sin(ang).astype(x.dtype)
    out = jnp.stack([x0 * cos - x1 * sin, x1 * cos + x0 * sin], axis=-2)
    return out.reshape(x.shape)


def block_attention(q, k, v, scale):
    B, Hk, G, T, dq = q.shape
    nb = T // Q_BLOCK
    qb = jnp.moveaxis(q.reshape(B, Hk, G, nb, Q_BLOCK, dq), 3, 0)

    def one_block(qblk):
        s = jnp.einsum('bhgqd,bhkd->bhgqk', qblk, k, preferred_element_type=jnp.float32) * scale
        p = jax.nn.softmax(s, axis=-1)
        return jnp.einsum('bhgqk,bhkd->bhgqd', p.astype(v.dtype), v)

    out = lax.map(one_block, qb)
    return jnp.moveaxis(out, 0, 3).reshape(B, Hk, G, T, v.shape[-1])


def merge_heads(o):
    B, T = o.shape[0], o.shape[3]
    return o.transpose(0, 3, 1, 2, 4).reshape(B, T, -1)


def short_conv(x, w):
    pad = w.shape[0] // 2
    return lax.conv_general_dilated(x, w[:, None, :].astype(x.dtype), window_strides=(1,), padding=[(pad, pad)],
                                    dimension_numbers=('NWC', 'WIO', 'NWC'), feature_group_count=x.shape[-1])


def mla_q(cq_pre, q_norm, w_uq, ang):
    B, T = cq_pre.shape[:2]
    q = (rms_norm(cq_pre, q_norm) @ w_uq).reshape(B, T, MLA_HEADS, MLA_NOPE + MLA_ROPE).transpose(0, 2, 1, 3)
    q_pe = q[..., MLA_NOPE:]
    if ang is not None:
        q_pe = apply_rope(q_pe, ang)
    return jnp.concatenate([q[..., :MLA_NOPE], q_pe], axis=-1)[:, :, None]


def mla_kv(ckv_pre, k_pe, kv_norm, w_ukv, ang):
    B, T = ckv_pre.shape[:2]
    kv = (rms_norm(ckv_pre, kv_norm) @ w_ukv).reshape(B, T, MLA_HEADS, MLA_NOPE + MLA_V).transpose(0, 2, 1, 3)
    k_pe = k_pe[:, None]
    if ang is not None:
        k_pe = apply_rope(k_pe, ang)
    k_pe = jnp.broadcast_to(k_pe, (B, MLA_HEADS, T, MLA_ROPE))
    k = jnp.concatenate([kv[..., :MLA_NOPE], k_pe], axis=-1)
    return k, kv[..., MLA_NOPE:]


def mla_mixer(px, pc, q_norm, w_uq, kv_norm, w_ukv, ang, with_ctx):
    scale = (MLA_NOPE + MLA_ROPE) ** -0.5
    kc, vc = mla_kv(pc[1], pc[2], kv_norm, w_ukv, None)
    kx, vx = mla_kv(px[1], px[2], kv_norm, w_ukv, ang)
    qx = mla_q(px[0], q_norm, w_uq, ang)
    out_x = merge_heads(block_attention(qx, jnp.concatenate([kc, kx], axis=2), jnp.concatenate([vc, vx], axis=2), scale))
    out_c = merge_heads(block_attention(mla_q(pc[0], q_norm, w_uq, None), kc, vc, scale)) if with_ctx else None
    return out_x, out_c


def mlstm_scan(q, k, v, log_i, log_f, state):
    B, H, T, dv = v.shape
    nc = T // ML_CHUNK

    def chunks(a):
        return jnp.moveaxis(a.reshape((B, H, nc, ML_CHUNK) + a.shape[3:]), 2, 0)

    lower = jnp.tril(jnp.ones((ML_CHUNK, ML_CHUNK), dtype=bool))

    def step(carry, xs):
        C, n, m = carry
        qc, kc, vc, li, lf = xs
        b = jnp.cumsum(lf, axis=-1)
        logw = jnp.where(lower, b[..., :, None] - b[..., None, :] + li[..., None, :], -jnp.inf)
        m_t = jnp.maximum(b + m[..., None], jnp.max(logw, axis=-1))
        w_state = jnp.exp(b + m[..., None] - m_t)
        s = jnp.einsum('bhtd,bhsd->bhts', qc, kc) * jnp.exp(logw - m_t[..., None])
        num = w_state[..., None] * jnp.einsum('bhtd,bhde->bhte', qc, C) + jnp.einsum('bhts,bhse->bhte', s, vc)
        den = w_state * jnp.einsum('bhtd,bhd->bht', qc, n) + jnp.sum(s, axis=-1)
        h = num / jnp.maximum(jnp.abs(den), jnp.exp(-m_t))[..., None]
        b_end = b[..., -1]
        g = b_end[..., None] - b + li
        m_new = jnp.maximum(b_end + m, jnp.max(g, axis=-1))
        decay = jnp.exp(b_end + m - m_new)
        wk = jnp.exp(g - m_new[..., None])[..., None] * kc
        C_new = decay[..., None, None] * C + jnp.einsum('bhsd,bhse->bhde', wk, vc)
        n_new = decay[..., None] * n + jnp.sum(wk, axis=2)
        return (C_new, n_new, m_new), h

    state, hs = lax.scan(step, state, (chunks(q), chunks(k), chunks(v), chunks(log_i), chunks(log_f)))
    return jnp.moveaxis(hs, 0, 2).reshape(B, H, T, dv), state


def mlstm_prep(p, conv_w, gate_b):
    qk_pre, v, o_pre, g = p
    B, T = v.shape[:2]
    qk = jax.nn.silu(short_conv(qk_pre, conv_w))
    q, k = jnp.split(qk, 2, axis=-1)

    def heads(a):
        return a.reshape(B, T, ML_HEADS, -1).transpose(0, 2, 1, 3).astype(jnp.float32)

    g = (g + gate_b).astype(jnp.float32).reshape(B, T, 2, 2, ML_HEADS).transpose(2, 3, 0, 4, 1)
    log_i = g[:, 0]
    log_f = jax.nn.log_sigmoid(g[:, 1])
    return heads(q) * ML_DK ** -0.5, heads(k), heads(v), o_pre, log_i, log_f


def mlstm_mixer(px, pc, conv_w, gate_b, head_norm, with_ctx):
    qx, kx, vx, ox, lix, lfx = mlstm_prep(px, conv_w, gate_b)
    qc, kc, vc, oc, lic, lfc = mlstm_prep(pc, conv_w, gate_b)
    B = qx.shape[0]
    zero = (jnp.zeros((B, ML_HEADS, ML_DK, ML_DV), jnp.float32), jnp.zeros((B, ML_HEADS, ML_DK), jnp.float32),
            jnp.zeros((B, ML_HEADS), jnp.float32))

    def rev(a):
        return jnp.flip(a, axis=2)

    hc_f, st_f = mlstm_scan(qc, kc, vc, lic[0], lfc[0], zero)
    hx_f, _ = mlstm_scan(qx, kx, vx, lix[0], lfx[0], st_f)
    hc_b, st_b = mlstm_scan(rev(qc), rev(kc), rev(vc), rev(lic[1]), rev(lfc[1]), zero)
    hx_b, _ = mlstm_scan(rev(qx), rev(kx), rev(vx), rev(lix[1]), rev(lfx[1]), st_b)

    def readout(hf, hb, o_pre):
        h = rms_norm(hf + rev(hb), head_norm.reshape(ML_HEADS, 1, ML_DV))
        return (jax.nn.sigmoid(o_pre) * merge_heads(h[:, :, None])).astype(o_pre.dtype)

    out_x = readout(hx_f, hx_b, ox)
    out_c = readout(hc_f, hc_b, oc) if with_ctx else None
    return out_x, out_c


def gqa_q(pq, q_norm, ang):
    B, T = pq.shape[:2]
    q = rms_norm(pq.reshape(B, T, GQA_KV_HEADS, GQA_GROUP, GQA_DH).transpose(0, 2, 3, 1, 4), q_norm)
    return q if ang is None else apply_rope(q, ang)


def gqa_kv(pk, pv, k_norm, ang):
    B, T = pk.shape[:2]

    def heads(a):
        return a.reshape(B, T, GQA_KV_HEADS, GQA_DH).transpose(0, 2, 1, 3)

    k = rms_norm(heads(pk), k_norm)
    if ang is not None:
        k = apply_rope(k, ang)
    return k, heads(pv)


def gqa_mixer(hx, hc, q_norm, k_norm, ang, with_ctx):
    scale = GQA_DH ** -0.5
    qx_pre, kx_pre, vx_pre = jnp.split(hx, ODD_SPLITS, axis=-1)
    qc_pre, kc_pre, vc_pre = jnp.split(hc, ODD_SPLITS, axis=-1)
    kc, vc = gqa_kv(kc_pre, vc_pre, k_norm, None)
    kx, vx = gqa_kv(kx_pre, vx_pre, k_norm, ang)
    qx = gqa_q(qx_pre, q_norm, ang)
    out_x = merge_heads(block_attention(qx, jnp.concatenate([kc, kx], axis=2), jnp.concatenate([vc, vx], axis=2), scale))
    out_c = merge_heads(block_attention(gqa_q(qc_pre, q_norm, None), kc, vc, scale)) if with_ctx else None
    return out_x, out_c


def finish_sublayers(s, mix, mods, w_out, norm2, w1, w2):
    s = s + mods[2][:, None] * (mix @ w_out)
    h = modulate(s, norm2, mods[3], mods[4])
    return s + mods[5][:, None] * (jnp.square(jax.nn.relu(h @ w1)) @ w2)


def even_layer(x, ctx, c, c_ctx, prm, ang, update_ctx):
    (ada_w, ada_b, norm1, w_in, mla_q_norm, mla_w_uq, mla_kv_norm, mla_w_ukv,
     ml_conv, ml_gate_b, ml_head_norm, w_out, norm2, w1, w2) = prm
    mx = adaln(c, ada_w, ada_b)
    mc = adaln(c_ctx[None], ada_w, ada_b)
    px = jnp.split(modulate(x, norm1, mx[0], mx[1]) @ w_in, EVEN_SPLITS, axis=-1)
    pc = jnp.split(modulate(ctx, norm1, mc[0], mc[1]) @ w_in, EVEN_SPLITS, axis=-1)
    a_x, a_c = mla_mixer(px[:3], pc[:3], mla_q_norm, mla_w_uq, mla_kv_norm, mla_w_ukv, ang, update_ctx)
    b_x, b_c = mlstm_mixer(px[3:], pc[3:], ml_conv, ml_gate_b, ml_head_norm, update_ctx)
    x = finish_sublayers(x, jnp.concatenate([a_x, b_x], axis=-1), mx, w_out, norm2, w1, w2)
    if update_ctx:
        ctx = finish_sublayers(ctx, jnp.concatenate([a_c, b_c], axis=-1), mc, w_out, norm2, w1, w2)
    return x, ctx


def odd_layer(x, ctx, c, c_ctx, prm, ang, update_ctx):
    (ada_w, ada_b, norm1, w_in, q_norm, k_norm, w_out, norm2, w1, w2) = prm
    mx = adaln(c, ada_w, ada_b)
    mc = adaln(c_ctx[None], ada_w, ada_b)
    hx = modulate(x, norm1, mx[0], mx[1]) @ w_in
    hc = modulate(ctx, norm1, mc[0], mc[1]) @ w_in
    att_x, att_c = gqa_mixer(hx, hc, q_norm, k_norm, ang, update_ctx)
    x = finish_sublayers(x, att_x, mx, w_out, norm2, w1, w2)
    if update_ctx:
        ctx = finish_sublayers(ctx, att_c, mc, w_out, norm2, w1, w2)
    return x, ctx


def setup_inputs(seed: int = 0) -> dict:
    key = jax.random.key(seed)
    ks = iter(jax.random.split(key, 40))

    def nrm(shape, s):
        return jax.random.normal(next(ks), shape, jnp.float32) * s

    def gain(n):
        return 1.0 + nrm((n,), 0.01)

    D = D_MODEL
    f_bias = jnp.linspace(ML_F_BIAS_LO, ML_F_BIAS_HI, ML_HEADS, dtype=jnp.float32)
    gate_b = nrm((2, 2, ML_HEADS), 0.01).at[:, 1].add(f_bias).reshape(-1)
    return {
        'x': nrm((BATCH, SEQ, D), 1.0),
        'c': nrm((BATCH, D), 1.0),
        'ctx': nrm((BATCH, CTX_LEN, D), 1.0),
        'c_ctx': nrm((D,), 1.0),
        'l0_ada_w': nrm((D, N_MOD * D), 0.5 * D ** -0.5),
        'l0_ada_b': nrm((N_MOD * D,), 0.01),
        'l0_norm1': gain(D),
        'l0_w_in': nrm((D, EVEN_IN), D ** -0.5),
        'l0_mla_q_norm': gain(MLA_Q_RANK),
        'l0_mla_w_uq': nrm((MLA_Q_RANK, MLA_HEADS * (MLA_NOPE + MLA_ROPE)), MLA_Q_RANK ** -0.5),
        'l0_mla_kv_norm': gain(MLA_KV_RANK),
        'l0_mla_w_ukv': nrm((MLA_KV_RANK, MLA_HEADS * (MLA_NOPE + MLA_V)), MLA_KV_RANK ** -0.5),
        'l0_ml_conv': nrm((ML_CONV, 2 * ML_HEADS * ML_DK), ML_CONV ** -0.5),
        'l0_ml_gate_b': gate_b,
        'l0_ml_head_norm': gain(ML_HEADS * ML_DV),
        'l0_w_out': nrm((EVEN_MIX, D), EVEN_MIX ** -0.5),
        'l0_norm2': gain(D),
        'l0_w1': nrm((D, D_FF), D ** -0.5),
        'l0_w2': nrm((D_FF, D), D_FF ** -0.5),
        'l1_ada_w': nrm((D, N_MOD * D), 0.5 * D ** -0.5),
        'l1_ada_b': nrm((N_MOD * D,), 0.01),
        'l1_norm1': gain(D),
        'l1_w_in': nrm((D, ODD_IN), D ** -0.5),
        'l1_q_norm': gain(GQA_DH),
        'l1_k_norm': gain(GQA_DH),
        'l1_w_out': nrm((ODD_MIX, D), ODD_MIX ** -0.5),
        'l1_norm2': gain(D),
        'l1_w1': nrm((D, D_FF), D ** -0.5),
        'l1_w2': nrm((D_FF, D), D_FF ** -0.5),
        'final_norm': gain(D),
    }


def reference(x, c, ctx, c_ctx,
              l0_ada_w, l0_ada_b, l0_norm1, l0_w_in, l0_mla_q_norm, l0_mla_w_uq, l0_mla_kv_norm, l0_mla_w_ukv,
              l0_ml_conv, l0_ml_gate_b, l0_ml_head_norm, l0_w_out, l0_norm2, l0_w1, l0_w2,
              l1_ada_w, l1_ada_b, l1_norm1, l1_w_in, l1_q_norm, l1_k_norm, l1_w_out, l1_norm2, l1_w1, l1_w2,
              final_norm):
    T = x.shape[1]
    ang_mla = grid_angles(T, MLA_ROPE)
    ang_gqa = grid_angles(T, GQA_DH)
    layers = (
        (even_layer, (l0_ada_w, l0_ada_b, l0_norm1, l0_w_in, l0_mla_q_norm, l0_mla_w_uq, l0_mla_kv_norm, l0_mla_w_ukv,
                      l0_ml_conv, l0_ml_gate_b, l0_ml_head_norm, l0_w_out, l0_norm2, l0_w1, l0_w2), ang_mla),
        (odd_layer, (l1_ada_w, l1_ada_b, l1_norm1, l1_w_in, l1_q_norm, l1_k_norm, l1_w_out, l1_norm2, l1_w1, l1_w2), ang_gqa),
    )
    for i in range(DEPTH):
        layer_fn, prm, ang = layers[i]
        x, ctx = layer_fn(x, ctx, c, c_ctx, prm, ang, i < DEPTH - 1)
    return rms_norm(x, final_norm)
```

```python
import functools

import numpy as np
import jax
import jax.numpy as jnp
from jax import lax
from jax.experimental import pallas as pl
from jax.experimental.pallas import tpu as pltpu

F32 = jnp.float32
BF16 = jnp.bfloat16

GRID_W = 64
NORM_EPS = 1e-6
ROPE_THETA = 10000.0
N_MOD = 6
MLA_HEADS = 8
MLA_Q_RANK = 256
MLA_KV_RANK = 128
MLA_NOPE = 64
MLA_ROPE = 32
MLA_V = 64
ML_HEADS = 4
ML_DK = 128
ML_DV = 128
ML_CHUNK = 128
N_GATES = 4 * ML_HEADS
GQA_HEADS = 8
GQA_KV_HEADS = 2
GQA_GROUP = GQA_HEADS // GQA_KV_HEADS
GQA_DH = 128

LANES = 128
V7X_VMEM_BYTES = 64 * 1024 * 1024
VMEM_LIMIT = V7X_VMEM_BYTES * 7 // 8

ROW_TILE = 256
ATT_TQ = 256
ATT_TK = 512
KPE_OFF = 0
GATE_OFF = 32


def _params(sem):
    return pltpu.CompilerParams(dimension_semantics=sem, vmem_limit_bytes=VMEM_LIMIT)


def _rms(x, g):
    var = jnp.mean(x * x, axis=-1, keepdims=True)
    return x * lax.rsqrt(var + NORM_EPS) * g


def _const_spec(shape):
    nd = len(shape)
    return pl.BlockSpec(shape, lambda *_: (0,) * nd)


def _adaln_kernel(c_ref, w_ref, b_ref, o_ref):
    c = c_ref[...]
    a = c * jax.nn.sigmoid(c)
    o_ref[...] = jnp.dot(a.astype(BF16), w_ref[...].astype(BF16),
                         preferred_element_type=F32) + b_ref[...]


def _adaln(cv, w, b):
    d = cv.shape[1]
    n = w.shape[1]
    tn = n // 4
    return pl.pallas_call(
        _adaln_kernel,
        grid=(n // tn,),
        in_specs=[pl.BlockSpec((8, d), lambda j: (0, 0)),
                  pl.BlockSpec((d, tn), lambda j: (0, j)),
                  pl.BlockSpec((1, tn), lambda j: (0, j))],
        out_specs=pl.BlockSpec((8, tn), lambda j: (0, j)),
        out_shape=jax.ShapeDtypeStruct((8, n), F32),
        compiler_params=_params(("arbitrary",)),
        name="adaln",
    )(cv, w, b.reshape(1, n))


def _row_is_ctx(tile, tm, t_len):
    row = tile * tm + lax.broadcasted_iota(jnp.int32, (tm, 1), 0)
    return row >= t_len


def _inproj_kernel(h_ref, g_ref, mx_ref, mc_ref, w_ref, *out_refs, tm, t_len, widths):
    is_ctx = _row_is_ctx(pl.program_id(1), tm, t_len)
    shift = jnp.where(is_ctx, mc_ref[0:1, :], mx_ref[0, 0:1, :])
    scale = jnp.where(is_ctx, mc_ref[1:2, :], mx_ref[0, 1:2, :])
    y = (_rms(h_ref[0], g_ref[...]) * (1.0 + scale) + shift).astype(BF16)
    off = 0
    for o_ref, width in zip(out_refs, widths):
        o_ref[0] = jnp.dot(y, w_ref[:, off:off + width],
                           preferred_element_type=F32).astype(o_ref.dtype)
        off += width


def _inproj(h, g, modsx, modsc, w, widths, dtypes, t_len):
    bsz, s_len, d = h.shape
    tm = ROW_TILE
    kern = functools.partial(_inproj_kernel, tm=tm, t_len=t_len, widths=widths)
    return pl.pallas_call(
        kern,
        grid=(bsz, s_len // tm),
        in_specs=[pl.BlockSpec((1, tm, d), lambda b, i: (b, i, 0)),
                  _const_spec((1, d)),
                  pl.BlockSpec((1, 8, d), lambda b, i: (b, 0, 0)),
                  _const_spec((8, d)),
                  _const_spec(w.shape)],
        out_specs=[pl.BlockSpec((1, tm, wd), lambda b, i: (b, i, 0)) for wd in widths],
        out_shape=[jax.ShapeDtypeStruct((bsz, s_len, wd), dt) for wd, dt in zip(widths, dtypes)],
        compiler_params=_params(("parallel", "parallel")),
        name="inproj",
    )(h, g.reshape(1, d), modsx, modsc, w)


def _rope(x, c, s1, s2, half):
    return x * c + pltpu.roll(x, half, 1) * s1 + pltpu.roll(x, LANES - half, 1) * s2


def _mla_prep_kernel(lat_ref, side_ref, c_ref, s1_ref, s2_ref, qn_ref, kvn_ref,
                     wq_ref, wk_ref, wv_ref, q_ref, k_ref, v_ref, *, scale):
    lat = lat_ref[0].astype(F32)
    cqn = _rms(lat[:, :MLA_Q_RANK], qn_ref[...]).astype(BF16)
    ckvn = _rms(lat[:, MLA_Q_RANK:], kvn_ref[...]).astype(BF16)
    c, s1, s2 = c_ref[...], s1_ref[...], s2_ref[...]
    q = jnp.dot(cqn, wq_ref[...], preferred_element_type=F32)
    kin = jnp.concatenate([ckvn, side_ref[0].astype(BF16)], axis=-1)
    k = jnp.dot(kin, wk_ref[...], preferred_element_type=F32)
    half = MLA_ROPE // 2
    for h in range(MLA_HEADS):
        sl = slice(h * LANES, (h + 1) * LANES)
        q_ref[0, :, sl] = (_rope(q[:, sl], c, s1, s2, half) * scale).astype(BF16)
        k_ref[0, :, sl] = _rope(k[:, sl], c, s1, s2, half).astype(BF16)
    v_ref[0] = jnp.dot(ckvn, wv_ref[...], preferred_element_type=F32).astype(BF16)


def _mla_prep(lat, side, tabs, qn, kvn, wq, wk, wv):
    bsz, s_len, lw = lat.shape
    tm = ROW_TILE
    hw = MLA_HEADS * LANES
    vw = MLA_HEADS * MLA_V
    tok = lambda w: pl.BlockSpec((1, tm, w), lambda b, i: (b, i, 0))
    tab = pl.BlockSpec((tm, LANES), lambda b, i: (i, 0))
    kern = functools.partial(_mla_prep_kernel, scale=(MLA_NOPE + MLA_ROPE) ** -0.5)
    return pl.pallas_call(
        kern,
        grid=(bsz, s_len // tm),
        in_specs=[tok(lw), tok(LANES), tab, tab, tab,
                  _const_spec((1, MLA_Q_RANK)), _const_spec((1, MLA_KV_RANK)),
                  _const_spec(wq.shape), _const_spec(wk.shape), _const_spec(wv.shape)],
        out_specs=[tok(hw), tok(hw), tok(vw)],
        out_shape=[jax.ShapeDtypeStruct((bsz, s_len, hw), BF16),
                   jax.ShapeDtypeStruct((bsz, s_len, hw), BF16),
                   jax.ShapeDtypeStruct((bsz, s_len, vw), BF16)],
        compiler_params=_params(("parallel", "parallel")),
        name="mla_prep",
    )(lat, side, *tabs, qn.reshape(1, -1), kvn.reshape(1, -1), wq, wk, wv)


def _gqa_prep_kernel(q_in, k_in, c_ref, s1_ref, s2_ref, qn_ref, kn_ref, q_ref, k_ref, *, scale):
    c, s1, s2 = c_ref[...], s1_ref[...], s2_ref[...]
    half = GQA_DH // 2
    for h in range(GQA_HEADS):
        sl = slice(h * LANES, (h + 1) * LANES)
        x = _rms(q_in[0, :, sl].astype(F32), qn_ref[...])
        q_ref[0, :, sl] = (_rope(x, c, s1, s2, half) * scale).astype(BF16)
    for h in range(GQA_KV_HEADS):
        sl = slice(h * LANES, (h + 1) * LANES)
        x = _rms(k_in[0, :, sl].astype(F32), kn_ref[...])
        k_ref[0, :, sl] = _rope(x, c, s1, s2, half).astype(BF16)


def _gqa_prep(hx, tabs, qn, kn):
    bsz, s_len, _ = hx.shape
    tm = ROW_TILE
    qw = GQA_HEADS * GQA_DH
    kw = GQA_KV_HEADS * GQA_DH
    tab = pl.BlockSpec((tm, LANES), lambda b, i: (i, 0))
    kern = functools.partial(_gqa_prep_kernel, scale=GQA_DH ** -0.5)
    return pl.pallas_call(
        kern,
        grid=(bsz, s_len // tm),
        in_specs=[pl.BlockSpec((1, tm, qw), lambda b, i: (b, i, 0)),
                  pl.BlockSpec((1, tm, kw), lambda b, i: (b, i, qw // kw)),
                  tab, tab, tab, _const_spec((1, GQA_DH)), _const_spec((1, GQA_DH))],
        out_specs=[pl.BlockSpec((1, tm, qw), lambda b, i: (b, i, 0)),
                   pl.BlockSpec((1, tm, kw), lambda b, i: (b, i, 0))],
        out_shape=[jax.ShapeDtypeStruct((bsz, s_len, qw), BF16),
                   jax.ShapeDtypeStruct((bsz, s_len, kw), BF16)],
        compiler_params=_params(("parallel", "parallel")),
        name="gqa_prep",
    )(hx, hx, *tabs, qn.reshape(1, -1), kn.reshape(1, -1))


def _attn_kernel(q_ref, k_ref, v_ref, o_ref, m_sc, l_sc, acc_sc, *,
                 n_heads, k_shared, v_pair, tq, tk, t_len, s_len):
    qi = pl.program_id(2)

    def scores(g, kc):
        qg = q_ref[0, :, g * LANES:(g + 1) * LANES]
        kg = kc if k_shared else kc[:, g * LANES:(g + 1) * LANES]
        return lax.dot_general(qg, kg, (((1,), (1,)), ((), ())), preferred_element_type=F32)

    kc = k_ref[0, t_len:s_len, :]
    vc = v_ref[0, t_len:s_len, :]
    for g in range(n_heads):
        s = scores(g, kc)
        m = jnp.max(s, axis=-1, keepdims=True)
        p = jnp.exp(s - m)
        m_sc[g] = m
        l_sc[g] = jnp.sum(p, axis=-1, keepdims=True)
        acc_sc[g] = jnp.dot(p.astype(BF16), vc, preferred_element_type=F32)

    def body(j, carry):
        start = pl.multiple_of(j * tk, tk)
        kc = k_ref[0, pl.ds(start, tk), :]
        vc = v_ref[0, pl.ds(start, tk), :]
        for g in range(n_heads):
            s = scores(g, kc)
            m_old = m_sc[g]
            m_new = jnp.maximum(m_old, jnp.max(s, axis=-1, keepdims=True))
            alpha = jnp.exp(m_old - m_new)
            p = jnp.exp(s - m_new)
            l_sc[g] = alpha * l_sc[g] + jnp.sum(p, axis=-1, keepdims=True)
            acc_sc[g] = alpha * acc_sc[g] + jnp.dot(p.astype(BF16), vc, preferred_element_type=F32)
            m_sc[g] = m_new
        return carry

    n_latent = jnp.where(qi * tq < t_len, t_len // tk, 0)
    lax.fori_loop(0, n_latent, body, 0)

    outs = [acc_sc[g] / l_sc[g] for g in range(n_heads)]
    if v_pair:
        lane = lax.broadcasted_iota(jnp.int32, (tq, LANES), 1)
        o_ref[0] = jnp.where(lane < LANES // 2, outs[0], outs[1]).astype(o_ref.dtype)
    else:
        for g in range(n_heads):
            o_ref[0, :, g * LANES:(g + 1) * LANES] = outs[g].astype(o_ref.dtype)


def _attention(q, k, v, *, n_groups, n_heads, k_shared, v_pair, v_col0, n_q_rows, t_len):
    bsz, s_len, _ = q.shape
    tq, tk = ATT_TQ, ATT_TK
    qw = n_heads * LANES
    kw = LANES if k_shared else qw
    ow = LANES if v_pair else qw
    kern = functools.partial(_attn_kernel, n_heads=n_heads, k_shared=k_shared, v_pair=v_pair,
                             tq=tq, tk=tk, t_len=t_len, s_len=s_len)
    return pl.pallas_call(
        kern,
        grid=(bsz, n_groups, n_q_rows // tq),
        in_specs=[pl.BlockSpec((1, tq, qw), lambda b, g, i: (b, i, g)),
                  pl.BlockSpec((1, s_len, kw), lambda b, g, i: (b, 0, g)),
                  pl.BlockSpec((1, s_len, LANES), lambda b, g, i: (b, 0, v_col0 + g))],
        out_specs=pl.BlockSpec((1, tq, ow), lambda b, g, i: (b, i, g)),
        out_shape=jax.ShapeDtypeStruct((bsz, n_q_rows, n_groups * ow), BF16),
        scratch_shapes=[pltpu.VMEM((n_heads, tq, 1), F32),
                        pltpu.VMEM((n_heads, tq, 1), F32),
                        pltpu.VMEM((n_heads, tq, LANES), F32)],
        compiler_params=_params(("parallel", "parallel", "arbitrary")),
        name="attention",
    )(q, k, v)


def _conv_kernel(cur_ref, prev_ref, next_ref, w_ref, o_ref, *, tm, t_len, s_len, qscale):
    i = pl.program_id(1)
    x = cur_ref[0].astype(F32)
    row = i * tm + lax.broadcasted_iota(jnp.int32, (tm, 1), 0)
    local = lax.broadcasted_iota(jnp.int32, (tm, 1), 0)
    prev_row = prev_ref[0, 15:16, :].astype(F32)
    next_row = next_ref[0, 0:1, :].astype(F32)
    xm1 = jnp.where(local == 0, prev_row, pltpu.roll(x, 1, 0))
    xp1 = jnp.where(local == tm - 1, next_row, pltpu.roll(x, tm - 1, 0))
    xm1 = jnp.where((row == 0) | (row == t_len), 0.0, xm1)
    xp1 = jnp.where((row == t_len - 1) | (row == s_len - 1), 0.0, xp1)
    y = w_ref[0:1, :] * xm1 + w_ref[1:2, :] * x + w_ref[2:3, :] * xp1
    y = y * jax.nn.sigmoid(y)
    half = y.shape[1] // 2
    o_ref[0, :, :half] = (y[:, :half] * qscale).astype(BF16)
    o_ref[0, :, half:] = y[:, half:].astype(BF16)


def _conv_silu(qk_pre, conv_w, t_len):
    bsz, s_len, w = qk_pre.shape
    tm = ROW_TILE
    r16 = tm // 16
    n16 = s_len // 16
    kern = functools.partial(_conv_kernel, tm=tm, t_len=t_len, s_len=s_len, qscale=ML_DK ** -0.5)
    return pl.pallas_call(
        kern,
        grid=(bsz, s_len // tm),
        in_specs=[pl.BlockSpec((1, tm, w), lambda b, i: (b, i, 0)),
                  pl.BlockSpec((1, 16, w), lambda b, i: (b, jnp.maximum(i * r16 - 1, 0), 0)),
                  pl.BlockSpec((1, 16, w), lambda b, i: (b, jnp.minimum((i + 1) * r16, n16 - 1), 0)),
                  _const_spec((8, w))],
        out_specs=pl.BlockSpec((1, tm, w), lambda b, i: (b, i, 0)),
        out_shape=jax.ShapeDtypeStruct((bsz, s_len, w), BF16),
        compiler_params=_params(("parallel", "parallel")),
        name="conv_silu",
    )(qk_pre, qk_pre, qk_pre, jnp.pad(conv_w, ((0, 8 - conv_w.shape[0]), (0, 0))))


def _split3(a):
    a1 = a.astype(BF16)
    r1 = a - a1.astype(F32)
    a2 = r1.astype(BF16)
    a3 = (r1 - a2.astype(F32)).astype(BF16)
    return a1, a2, a3


def _mlstm_kernel(qkf_ref, qkb_ref, vf_ref, vb_ref, gf_ref, gb_ref, bias_ref,
                  hf_ref, hb_ref, c_sc, m_sc):
    L = ML_CHUNK

    @pl.when(pl.program_id(1) == 0)
    def _():
        c_sc[...] = jnp.zeros_like(c_sc)
        m_sc[...] = jnp.zeros_like(m_sc)

    r = lax.broadcasted_iota(jnp.int32, (L, L), 0)
    c = lax.broadcasted_iota(jnp.int32, (L, L), 1)
    lane = lax.broadcasted_iota(jnp.int32, (1, LANES), 1) - GATE_OFF
    is_forget = (lane >= 0) & (lane < N_GATES) & ((lane % (2 * ML_HEADS)) >= ML_HEADS)
    ones = jnp.ones((L, ML_DV), BF16)
    dirs = ((qkf_ref, vf_ref, gf_ref, hf_ref), (qkb_ref, vb_ref, gb_ref, hb_ref))
    for d, (qk_ref, v_ref, g_ref, h_ref) in enumerate(dirs):
        seen = (c <= r) if d == 0 else (c >= r)
        seen_b = seen.astype(F32).astype(BF16)
        last = L - 1 if d == 0 else 0
        gates = g_ref[0] + bias_ref[...]
        soft = jnp.minimum(gates, 0.0) - jnp.log1p(jnp.exp(-jnp.abs(gates)))
        logs = jnp.where(is_forget, soft, gates)
        logs_t = logs.T[GATE_OFF:GATE_OFF + N_GATES, :]
        cum_col = sum(jnp.dot(seen_b, p, preferred_element_type=F32) for p in _split3(logs))
        cum_row = sum(lax.dot_general(p, seen_b, (((1,), (1,)), ((), ())), preferred_element_type=F32)
                      for p in _split3(logs_t))
        for h in range(ML_HEADS):
            gi = d * 2 * ML_HEADS + h
            gf = gi + ML_HEADS
            b_col = cum_col[:, GATE_OFF + gf:GATE_OFF + gf + 1]
            b_row = cum_row[gf:gf + 1, :]
            li_col = logs[:, GATE_OFF + gi:GATE_OFF + gi + 1]
            li_row = logs_t[gi:gi + 1, :]
            b_end = b_row[:, last:last + 1]
            m_prev = m_sc[d, h][0:1, 0:1]
            sl = slice(h * LANES, (h + 1) * LANES)
            q = qk_ref[0, :, sl]
            k = qk_ref[0, :, ML_HEADS * ML_DK + h * LANES:ML_HEADS * ML_DK + (h + 1) * LANES]
            vext = jnp.concatenate([v_ref[0, :, sl], ones], axis=-1)
            state = c_sc[d, h]

            logw = jnp.where(seen, b_col - b_row + li_row, -jnp.inf)
            m_t = jnp.maximum(b_col + m_prev, jnp.max(logw, axis=-1, keepdims=True))
            w_state = jnp.exp(b_col + m_prev - m_t)
            qk = lax.dot_general(q, k, (((1,), (1,)), ((), ())), preferred_element_type=F32)
            s = qk * jnp.exp(logw - m_t)
            inter = jnp.dot(q, state.astype(BF16), preferred_element_type=F32)
            intra = jnp.dot(s.astype(BF16), vext, preferred_element_type=F32)
            mixed = w_state * inter + intra
            den = jnp.maximum(jnp.abs(mixed[:, ML_DV:]), jnp.exp(-m_t))
            h_ref[0, :, sl] = mixed[:, :ML_DV] / den

            g_col = b_end - b_col + li_col
            m_new = jnp.maximum(b_end + m_prev, jnp.max(g_col, axis=0, keepdims=True))
            decay = jnp.exp(b_end + m_prev - m_new)
            wk = (jnp.exp(g_col - m_new) * k.astype(F32)).astype(BF16)
            upd = lax.dot_general(wk, vext, (((0,), (0,)), ((), ())), preferred_element_type=F32)
            c_sc[d, h] = decay * state + upd
            m_sc[d, h] = jnp.broadcast_to(m_new, (8, LANES))


def _mlstm(qk, v, side, gate_bias_row, t_len):
    bsz, s_len, _ = qk.shape
    L = ML_CHUNK
    n_chunks = s_len // L
    n_lat = t_len // L
    hw = ML_HEADS * ML_DV
    fwd = lambda b, j: (b, (j + n_lat) % n_chunks, 0)
    bwd = lambda b, j: (b, n_chunks - 1 - j, 0)
    return pl.pallas_call(
        _mlstm_kernel,
        grid=(bsz, n_chunks),
        in_specs=[pl.BlockSpec((1, L, 2 * hw), fwd), pl.BlockSpec((1, L, 2 * hw), bwd),
                  pl.BlockSpec((1, L, hw), fwd), pl.BlockSpec((1, L, hw), bwd),
                  pl.BlockSpec((1, L, LANES), fwd), pl.BlockSpec((1, L, LANES), bwd),
                  _const_spec((1, LANES))],
        out_specs=[pl.BlockSpec((1, L, hw), fwd), pl.BlockSpec((1, L, hw), bwd)],
        out_shape=[jax.ShapeDtypeStruct((bsz, s_len, hw), F32)] * 2,
        scratch_shapes=[pltpu.VMEM((2, ML_HEADS, ML_DK, 2 * ML_DV), F32),
                        pltpu.VMEM((2, ML_HEADS, 8, LANES), F32)],
        compiler_params=_params(("parallel", "arbitrary")),
        name="mlstm",
    )(qk, qk, v, v, side, side, gate_bias_row)


def _finish_kernel(*refs, tm, t_len, with_readout, with_final, ff_chunk):
    it = iter(refs)
    h_ref = next(it)
    if with_readout:
        a_ref, hf_ref, hb_ref, o_ref, hn_ref = (next(it) for _ in range(5))
    else:
        mix_ref = next(it)
    mx_ref, mc_ref, wo_ref, n2_ref, w1_ref, w2_ref = (next(it) for _ in range(6))
    fn_ref = next(it) if with_final else None
    out_ref = next(it)

    is_ctx = _row_is_ctx(pl.program_id(1), tm, t_len)
    mod = lambda j: jnp.where(is_ctx, mc_ref[j:j + 1, :], mx_ref[0, j:j + 1, :])

    if with_readout:
        hsum = hf_ref[0] + hb_ref[0]
        parts = [_rms(hsum[:, h * ML_DV:(h + 1) * ML_DV], hn_ref[:, h * ML_DV:(h + 1) * ML_DV])
                 for h in range(ML_HEADS)]
        rec = jax.nn.sigmoid(o_ref[0].astype(F32)) * jnp.concatenate(parts, axis=-1)
        mix = jnp.concatenate([a_ref[0], rec.astype(BF16)], axis=-1)
    else:
        mix = mix_ref[0]
    s = h_ref[0] + mod(2) * jnp.dot(mix, wo_ref[...], preferred_element_type=F32)
    y = (_rms(s, n2_ref[...]) * (1.0 + mod(4)) + mod(3)).astype(BF16)
    d_ff = w1_ref.shape[1]
    acc = jnp.zeros(s.shape, F32)
    for c0 in range(0, d_ff, ff_chunk):
        u = jnp.maximum(jnp.dot(y, w1_ref[:, c0:c0 + ff_chunk], preferred_element_type=F32), 0.0)
        acc = acc + jnp.dot((u * u).astype(BF16), w2_ref[c0:c0 + ff_chunk, :],
                            preferred_element_type=F32)
    out = s + mod(5) * acc
    if with_final:
        out = _rms(out, fn_ref[...])
    out_ref[0] = out


def _finish(h, mix_parts, modsx, modsc, w_out, norm2, w1, w2, *, t_len, n_rows, head_norm=None,
            final_norm=None):
    bsz, _, d = h.shape
    tm = ROW_TILE
    with_readout = head_norm is not None
    with_final = final_norm is not None
    tok = lambda w: pl.BlockSpec((1, tm, w), lambda b, i: (b, i, 0))
    once = lambda shape: pl.BlockSpec(shape, lambda *_: (0,) * len(shape),
                                      pipeline_mode=pl.Buffered(1))
    args = [h] + list(mix_parts)
    specs = [tok(d)] + [tok(p.shape[-1]) for p in mix_parts]
    if with_readout:
        args.append(head_norm.reshape(1, -1))
        specs.append(_const_spec((1, head_norm.shape[0])))
    args += [modsx, modsc, w_out, norm2.reshape(1, d), w1, w2]
    specs += [pl.BlockSpec((1, 8, d), lambda b, i: (b, 0, 0)), _const_spec((8, d)),
              once(w_out.shape), _const_spec((1, d)), once(w1.shape), once(w2.shape)]
    if with_final:
        args.append(final_norm.reshape(1, d))
        specs.append(_const_spec((1, d)))
    kern = functools.partial(_finish_kernel, tm=tm, t_len=t_len, with_readout=with_readout,
                             with_final=with_final, ff_chunk=1024)
    return pl.pallas_call(
        kern,
        grid=(bsz, n_rows // tm),
        in_specs=specs,
        out_specs=tok(d),
        out_shape=jax.ShapeDtypeStruct((bsz, n_rows, d), F32),
        compiler_params=_params(("parallel", "parallel")),
        name="finish",
    )(*args)


def _grid_angles(n_tokens, d_rot):
    rows = n_tokens // GRID_W
    row, col = jnp.meshgrid(jnp.arange(rows), jnp.arange(GRID_W), indexing="ij")
    pos = jnp.stack([row.reshape(-1), col.reshape(-1)], axis=-1).astype(F32)
    n_freq = d_rot // 4
    freqs = ROPE_THETA ** (-jnp.arange(n_freq, dtype=F32) / n_freq)
    return (pos[:, :, None] * freqs).reshape(n_tokens, 2 * n_freq)


def _rope_tables(ang, lane0, ctx_len):
    t_len, n = ang.shape
    cos, sin = jnp.cos(ang), jnp.sin(ang)
    c = jnp.ones((t_len, LANES), F32).at[:, lane0:lane0 + n].set(cos).at[:, lane0 + n:lane0 + 2 * n].set(cos)
    s1 = jnp.zeros((t_len, LANES), F32).at[:, lane0 + n:lane0 + 2 * n].set(sin)
    s2 = jnp.zeros((t_len, LANES), F32).at[:, lane0:lane0 + n].set(-sin)
    pad = lambda a, v: jnp.concatenate([a, jnp.full((ctx_len, LANES), v, F32)], axis=0)
    return pad(c, 1.0), pad(s1, 0.0), pad(s2, 0.0)


def _mla_rope_cols():
    nf = MLA_ROPE // 4
    first = [a * 2 * nf + f for a in range(2) for f in range(nf)]
    second = [a * 2 * nf + nf + f for a in range(2) for f in range(nf)]
    return np.array(first + second)


def _gqa_head_cols():
    nf = GQA_DH // 4
    first = [a * 2 * nf + f for a in range(2) for f in range(nf)]
    second = [a * 2 * nf + nf + f for a in range(2) for f in range(nf)]
    return np.array(first + second)


def kernel(x, c, ctx, c_ctx,
           l0_ada_w, l0_ada_b, l0_norm1, l0_w_in, l0_mla_q_norm, l0_mla_w_uq, l0_mla_kv_norm, l0_mla_w_ukv,
           l0_ml_conv, l0_ml_gate_b, l0_ml_head_norm, l0_w_out, l0_norm2, l0_w1, l0_w2,
           l1_ada_w, l1_ada_b, l1_norm1, l1_w_in, l1_q_norm, l1_k_norm, l1_w_out, l1_norm2, l1_w1, l1_w2,
           final_norm):
    bsz, t_len, d = x.shape
    ctx_len = ctx.shape[1]
    h = jnp.concatenate([x, ctx], axis=1)

    cv = jnp.zeros((8, d), F32).at[:bsz].set(c).at[bsz].set(c_ctx)

    def mods(ada_w, ada_b):
        m = _adaln(cv, ada_w, ada_b).reshape(8, N_MOD, d)
        pad = jnp.zeros((8 - N_MOD, d), F32)
        modsx = jnp.concatenate([m[:bsz], jnp.broadcast_to(pad, (bsz,) + pad.shape)], axis=1)
        modsc = jnp.concatenate([m[bsz], pad], axis=0)
        return modsx, modsc

    modsx, modsc = mods(l0_ada_w, l0_ada_b)
    o = np.cumsum([0, MLA_Q_RANK, MLA_KV_RANK, MLA_ROPE, 2 * ML_HEADS * ML_DK, ML_HEADS * ML_DV,
                   ML_HEADS * ML_DV, N_GATES])
    w_cq, w_ckv, w_kpe, w_qk, w_v, w_o, w_g = (l0_w_in[:, o[i]:o[i + 1]] for i in range(7))
    side_pad = jnp.zeros((d, LANES - MLA_ROPE - N_GATES), F32)
    w_in0 = jnp.concatenate([w_qk, w_v, w_o, w_cq, w_ckv, w_kpe[:, _mla_rope_cols()], w_g, side_pad],
                            axis=1).astype(BF16)
    widths0 = (2 * ML_HEADS * ML_DK, ML_HEADS * ML_DV, ML_HEADS * ML_DV, MLA_Q_RANK + MLA_KV_RANK, LANES)
    qk_pre, v_ml, o_pre, lat, side = _inproj(h, l0_norm1, modsx, modsc, w_in0, widths0,
                                             (BF16, BF16, BF16, BF16, F32), t_len)

    dq = MLA_NOPE + MLA_ROPE
    head_cols = np.concatenate([np.arange(MLA_NOPE), MLA_NOPE + _mla_rope_cols()])
    wq = l0_mla_w_uq.reshape(MLA_Q_RANK, MLA_HEADS, dq)[:, :, head_cols]
    wq = jnp.pad(wq, ((0, 0), (0, 0), (0, LANES - dq))).reshape(MLA_Q_RANK, MLA_HEADS * LANES).astype(BF16)
    wkv = l0_mla_w_ukv.reshape(MLA_KV_RANK, MLA_HEADS, MLA_NOPE + MLA_V)
    wk_nope = jnp.pad(wkv[:, :, :MLA_NOPE], ((0, 0), (0, 0), (0, LANES - MLA_NOPE)))
    place = np.zeros((LANES, MLA_HEADS, LANES), np.float32)
    for hh in range(MLA_HEADS):
        place[KPE_OFF + np.arange(MLA_ROPE), hh, MLA_NOPE + np.arange(MLA_ROPE)] = 1.0
    wk = jnp.concatenate([wk_nope, jnp.asarray(place)], axis=0).reshape(
        MLA_KV_RANK + LANES, MLA_HEADS * LANES).astype(BF16)
    wv = wkv[:, :, MLA_NOPE:].reshape(MLA_KV_RANK, MLA_HEADS * MLA_V).astype(BF16)
    tabs_mla = _rope_tables(_grid_angles(t_len, MLA_ROPE), MLA_NOPE, ctx_len)
    q_mla, k_mla, v_mla = _mla_prep(lat, side, tabs_mla, l0_mla_q_norm, l0_mla_kv_norm, wq, wk, wv)
    a_mix = _attention(q_mla, k_mla, v_mla, n_groups=MLA_HEADS // 2, n_heads=2, k_shared=False,
                       v_pair=True, v_col0=0, n_q_rows=t_len + ctx_len, t_len=t_len)

    qk = _conv_silu(qk_pre, l0_ml_conv, t_len)
    bias_row = jnp.zeros((1, LANES), F32).at[0, GATE_OFF:GATE_OFF + N_GATES].set(l0_ml_gate_b)
    hf, hb = _mlstm(qk, v_ml, side, bias_row, t_len)
    h = _finish(h, (a_mix, hf, hb, o_pre), modsx, modsc, l0_w_out.astype(BF16), l0_norm2,
                l0_w1.astype(BF16), l0_w2.astype(BF16), t_len=t_len, n_rows=t_len + ctx_len,
                head_norm=l0_ml_head_norm)

    modsx, modsc = mods(l1_ada_w, l1_ada_b)
    hc = _gqa_head_cols()
    n_qk = GQA_HEADS + GQA_KV_HEADS
    qk_cols = (np.arange(n_qk)[:, None] * GQA_DH + hc[None, :]).reshape(-1)
    w_in1 = jnp.concatenate([l1_w_in[:, qk_cols], l1_w_in[:, n_qk * GQA_DH:]], axis=1).astype(BF16)
    (hx,) = _inproj(h, l1_norm1, modsx, modsc, w_in1, (w_in1.shape[1],), (BF16,), t_len)
    tabs_gqa = _rope_tables(_grid_angles(t_len, GQA_DH), 0, ctx_len)
    q_gqa, k_gqa = _gqa_prep(hx, tabs_gqa, l1_q_norm[hc], l1_k_norm[hc])
    att = _attention(q_gqa, k_gqa, hx, n_groups=GQA_KV_HEADS, n_heads=GQA_GROUP, k_shared=True,
                     v_pair=False, v_col0=n_qk, n_q_rows=t_len, t_len=t_len)
    return _finish(h, (att,), modsx, modsc, l1_w_out.astype(BF16), l1_norm2,
                   l1_w1.astype(BF16), l1_w2.astype(BF16), t_len=t_len, n_rows=t_len,
                   final_norm=final_norm)
```

```python
import functools

import numpy as np
import jax
import jax.numpy as jnp
from jax import lax
from jax.experimental import pallas as pl
from jax.experimental.pallas import tpu as pltpu

F32 = jnp.float32
BF16 = jnp.bfloat16

GRID_W = 64
NORM_EPS = 1e-6
ROPE_THETA = 10000.0
N_MOD = 6
MLA_HEADS = 8
MLA_Q_RANK = 256
MLA_KV_RANK = 128
MLA_NOPE = 64
MLA_ROPE = 32
MLA_V = 64
ML_HEADS = 4
ML_DK = 128
ML_DV = 128
ML_CHUNK = 128
N_GATES = 4 * ML_HEADS
GQA_HEADS = 8
GQA_KV_HEADS = 2
GQA_GROUP = GQA_HEADS // GQA_KV_HEADS
GQA_DH = 128

LANES = 128
V7X_VMEM_BYTES = 64 * 1024 * 1024
VMEM_LIMIT = V7X_VMEM_BYTES * 7 // 8

ROW_TILE = 256
ATT_TQ = 512
ATT_TK = 256
KPE_OFF = 0
GATE_OFF = 32


def _params(sem):
    return pltpu.CompilerParams(dimension_semantics=sem, vmem_limit_bytes=VMEM_LIMIT)


def _rms(x, g):
    var = jnp.mean(x * x, axis=-1, keepdims=True)
    return x * lax.rsqrt(var + NORM_EPS) * g


def _const_spec(shape):
    nd = len(shape)
    return pl.BlockSpec(shape, lambda *_: (0,) * nd)


def _adaln_kernel(c_ref, w_ref, b_ref, o_ref):
    c = c_ref[...]
    a = c * jax.nn.sigmoid(c)
    o_ref[...] = jnp.dot(a.astype(BF16), w_ref[...].astype(BF16),
                         preferred_element_type=F32) + b_ref[...]


def _adaln(cv, w, b):
    d = cv.shape[1]
    n = w.shape[1]
    tn = n // 4
    return pl.pallas_call(
        _adaln_kernel,
        grid=(n // tn,),
        in_specs=[pl.BlockSpec((8, d), lambda j: (0, 0)),
                  pl.BlockSpec((d, tn), lambda j: (0, j)),
                  pl.BlockSpec((1, tn), lambda j: (0, j))],
        out_specs=pl.BlockSpec((8, tn), lambda j: (0, j)),
        out_shape=jax.ShapeDtypeStruct((8, n), F32),
        compiler_params=_params(("arbitrary",)),
        name="adaln",
    )(cv, w, b.reshape(1, n))


def _row_is_ctx(tile, tm, t_len):
    row = tile * tm + lax.broadcasted_iota(jnp.int32, (tm, 1), 0)
    return row >= t_len


def _inproj_kernel(h_ref, g_ref, mx_ref, mc_ref, w_ref, *out_refs, tm, t_len, widths):
    is_ctx = _row_is_ctx(pl.program_id(1), tm, t_len)
    shift = jnp.where(is_ctx, mc_ref[0:1, :], mx_ref[0, 0:1, :])
    scale = jnp.where(is_ctx, mc_ref[1:2, :], mx_ref[0, 1:2, :])
    y = (_rms(h_ref[0], g_ref[...]) * (1.0 + scale) + shift).astype(BF16)
    off = 0
    for o_ref, width in zip(out_refs, widths):
        o_ref[0] = jnp.dot(y, w_ref[:, off:off + width],
                           preferred_element_type=F32).astype(o_ref.dtype)
        off += width


def _inproj(h, g, modsx, modsc, w, widths, dtypes, t_len):
    bsz, s_len, d = h.shape
    tm = ROW_TILE
    kern = functools.partial(_inproj_kernel, tm=tm, t_len=t_len, widths=widths)
    return pl.pallas_call(
        kern,
        grid=(bsz, s_len // tm),
        in_specs=[pl.BlockSpec((1, tm, d), lambda b, i: (b, i, 0)),
                  _const_spec((1, d)),
                  pl.BlockSpec((1, 8, d), lambda b, i: (b, 0, 0)),
                  _const_spec((8, d)),
                  _const_spec(w.shape)],
        out_specs=[pl.BlockSpec((1, tm, wd), lambda b, i: (b, i, 0)) for wd in widths],
        out_shape=[jax.ShapeDtypeStruct((bsz, s_len, wd), dt) for wd, dt in zip(widths, dtypes)],
        compiler_params=_params(("parallel", "parallel")),
        name="inproj",
    )(h, g.reshape(1, d), modsx, modsc, w)


def _rope(x, c, s1, s2, half):
    return x * c + pltpu.roll(x, half, 1) * s1 + pltpu.roll(x, LANES - half, 1) * s2


LOG2E = 1.4426950408889634


def _mla_prep_kernel(lat_ref, side_ref, c_ref, s1_ref, s2_ref, qn_ref, kvn_ref,
                     wq_ref, wk_ref, wv_ref, qt_ref, k_ref, vt_ref, *, scale):
    lat = lat_ref[0].astype(F32)
    cqn = _rms(lat[:, :MLA_Q_RANK], qn_ref[...]).astype(BF16)
    ckvn = _rms(lat[:, MLA_Q_RANK:], kvn_ref[...]).astype(BF16)
    c, s1, s2 = c_ref[...], s1_ref[...], s2_ref[...]
    q = jnp.dot(cqn, wq_ref[...], preferred_element_type=F32)
    kin = jnp.concatenate([ckvn, side_ref[0].astype(BF16)], axis=-1)
    k = jnp.dot(kin, wk_ref[...], preferred_element_type=F32)
    half = MLA_ROPE // 2
    for h in range(MLA_HEADS):
        sl = slice(h * LANES, (h + 1) * LANES)
        qt_ref[0, sl, :] = (_rope(q[:, sl], c, s1, s2, half) * scale).T.astype(BF16)
        k_ref[0, :, sl] = _rope(k[:, sl], c, s1, s2, half).astype(BF16)
    vt_ref[0] = jnp.dot(ckvn, wv_ref[...], preferred_element_type=F32).T.astype(BF16)


def _mla_prep(lat, side, tabs, qn, kvn, wq, wk, wv):
    bsz, s_len, lw = lat.shape
    tm = ROW_TILE
    hw = MLA_HEADS * LANES
    vw = MLA_HEADS * MLA_V
    tok = lambda w: pl.BlockSpec((1, tm, w), lambda b, i: (b, i, 0))
    tok_t = lambda w: pl.BlockSpec((1, w, tm), lambda b, i: (b, 0, i))
    tab = pl.BlockSpec((tm, LANES), lambda b, i: (i, 0))
    kern = functools.partial(_mla_prep_kernel, scale=LOG2E * (MLA_NOPE + MLA_ROPE) ** -0.5)
    return pl.pallas_call(
        kern,
        grid=(bsz, s_len // tm),
        in_specs=[tok(lw), tok(LANES), tab, tab, tab,
                  _const_spec((1, MLA_Q_RANK)), _const_spec((1, MLA_KV_RANK)),
                  _const_spec(wq.shape), _const_spec(wk.shape), _const_spec(wv.shape)],
        out_specs=[tok_t(hw), tok(hw), tok_t(vw)],
        out_shape=[jax.ShapeDtypeStruct((bsz, hw, s_len), BF16),
                   jax.ShapeDtypeStruct((bsz, s_len, hw), BF16),
                   jax.ShapeDtypeStruct((bsz, vw, s_len), BF16)],
        compiler_params=_params(("parallel", "parallel")),
        name="mla_prep",
    )(lat, side, *tabs, qn.reshape(1, -1), kvn.reshape(1, -1), wq, wk, wv)


def _gqa_prep_kernel(q_in, k_in, v_in, c_ref, s1_ref, s2_ref, qn_ref, kn_ref,
                     qt_ref, k_ref, vt_ref, *, scale):
    c, s1, s2 = c_ref[...], s1_ref[...], s2_ref[...]
    half = GQA_DH // 2
    for h in range(GQA_HEADS):
        sl = slice(h * LANES, (h + 1) * LANES)
        x = _rms(q_in[0, :, sl].astype(F32), qn_ref[...])
        qt_ref[0, sl, :] = (_rope(x, c, s1, s2, half) * scale).T.astype(BF16)
    for h in range(GQA_KV_HEADS):
        sl = slice(h * LANES, (h + 1) * LANES)
        x = _rms(k_in[0, :, sl].astype(F32), kn_ref[...])
        k_ref[0, :, sl] = _rope(x, c, s1, s2, half).astype(BF16)
    vt_ref[0] = v_in[0].astype(F32).T.astype(BF16)


def _gqa_prep(hx, tabs, qn, kn):
    bsz, s_len, _ = hx.shape
    tm = ROW_TILE
    qw = GQA_HEADS * GQA_DH
    kw = GQA_KV_HEADS * GQA_DH
    tab = pl.BlockSpec((tm, LANES), lambda b, i: (i, 0))
    kern = functools.partial(_gqa_prep_kernel, scale=LOG2E * GQA_DH ** -0.5)
    return pl.pallas_call(
        kern,
        grid=(bsz, s_len // tm),
        in_specs=[pl.BlockSpec((1, tm, qw), lambda b, i: (b, i, 0)),
                  pl.BlockSpec((1, tm, kw), lambda b, i: (b, i, qw // kw)),
                  pl.BlockSpec((1, tm, kw), lambda b, i: (b, i, qw // kw + 1)),
                  tab, tab, tab, _const_spec((1, GQA_DH)), _const_spec((1, GQA_DH))],
        out_specs=[pl.BlockSpec((1, qw, tm), lambda b, i: (b, 0, i)),
                   pl.BlockSpec((1, tm, kw), lambda b, i: (b, i, 0)),
                   pl.BlockSpec((1, kw, tm), lambda b, i: (b, 0, i))],
        out_shape=[jax.ShapeDtypeStruct((bsz, qw, s_len), BF16),
                   jax.ShapeDtypeStruct((bsz, s_len, kw), BF16),
                   jax.ShapeDtypeStruct((bsz, kw, s_len), BF16)],
        compiler_params=_params(("parallel", "parallel")),
        name="gqa_prep",
    )(hx, hx, hx, *tabs, qn.reshape(1, -1), kn.reshape(1, -1))


def _attn_kernel(qt_ref, k_ref, vt_ref, o_ref, m_sc, l_sc, acc_sc, *,
                 n_heads, k_shared, v_pair, latent_queries, tq, tk, t_len, s_len):
    def scores(g, kc):
        kg = kc if k_shared else kc[:, g * LANES:(g + 1) * LANES]
        return jnp.dot(kg, qt_ref[0, g * LANES:(g + 1) * LANES, :], preferred_element_type=F32)

    kc = k_ref[0, t_len:s_len, :]
    vtc = vt_ref[0, :, t_len:s_len]
    for g in range(n_heads):
        st = scores(g, kc)
        m = jnp.max(st, axis=0, keepdims=True)
        p = jnp.exp2(st - m)
        m_sc[g] = jnp.broadcast_to(m, (8, tq))
        l_sc[g] = jnp.broadcast_to(jnp.sum(p, axis=0, keepdims=True), (8, tq))
        acc_sc[g] = jnp.dot(vtc, p.astype(BF16), preferred_element_type=F32)

    def body(j, carry):
        start = pl.multiple_of(j * tk, tk)
        kc = k_ref[0, pl.ds(start, tk), :]
        vtc = vt_ref[0, :, pl.ds(start, tk)]
        st_next = scores(0, kc)
        for g in range(n_heads):
            st = st_next
            if g + 1 < n_heads:
                st_next = scores(g + 1, kc)
            m_old = m_sc[g][0:1]
            m_new = jnp.maximum(m_old, jnp.max(st, axis=0, keepdims=True))
            alpha = jnp.exp2(m_old - m_new)
            p = jnp.exp2(st - m_new)
            l_new = alpha * l_sc[g][0:1] + jnp.sum(p, axis=0, keepdims=True)
            acc_sc[g] = alpha * acc_sc[g] + jnp.dot(vtc, p.astype(BF16), preferred_element_type=F32)
            l_sc[g] = jnp.broadcast_to(l_new, (8, tq))
            m_sc[g] = jnp.broadcast_to(m_new, (8, tq))
        return carry

    if latent_queries:
        lax.fori_loop(0, t_len // tk, body, 0)

    outs = [acc_sc[g] / l_sc[g][0:1] for g in range(n_heads)]
    if v_pair:
        row = lax.broadcasted_iota(jnp.int32, (LANES, tq), 0)
        o_ref[0] = jnp.where(row < LANES // 2, outs[0], outs[1]).T.astype(o_ref.dtype)
    else:
        for g in range(n_heads):
            o_ref[0, :, g * LANES:(g + 1) * LANES] = outs[g].T.astype(o_ref.dtype)


def _attention(qt, k, vt, *, n_groups, n_heads, k_shared, v_pair, t_len, latent_queries):
    bsz, s_len, _ = k.shape
    tq = ATT_TQ if latent_queries else s_len - t_len
    n_q_rows = t_len if latent_queries else s_len - t_len
    q0 = 0 if latent_queries else t_len // tq
    tk = ATT_TK
    qw = n_heads * LANES
    kw = LANES if k_shared else qw
    ow = LANES if v_pair else qw
    kern = functools.partial(_attn_kernel, n_heads=n_heads, k_shared=k_shared, v_pair=v_pair,
                             latent_queries=latent_queries, tq=tq, tk=tk, t_len=t_len, s_len=s_len)
    return pl.pallas_call(
        kern,
        grid=(bsz, n_groups, n_q_rows // tq),
        in_specs=[pl.BlockSpec((1, qw, tq), lambda b, g, i: (b, g, q0 + i)),
                  pl.BlockSpec((1, s_len, kw), lambda b, g, i: (b, 0, g)),
                  pl.BlockSpec((1, LANES, s_len), lambda b, g, i: (b, g, 0))],
        out_specs=pl.BlockSpec((1, tq, ow), lambda b, g, i: (b, i, g)),
        out_shape=jax.ShapeDtypeStruct((bsz, n_q_rows, n_groups * ow), BF16),
        scratch_shapes=[pltpu.VMEM((n_heads, 8, tq), F32),
                        pltpu.VMEM((n_heads, 8, tq), F32),
                        pltpu.VMEM((n_heads, LANES, tq), F32)],
        compiler_params=_params(("parallel", "parallel", "arbitrary")),
        name="attention",
    )(qt, k, vt)


def _conv_kernel(cur_ref, prev_ref, next_ref, w_ref, o_ref, *, tm, t_len, s_len, qscale):
    i = pl.program_id(1)
    x = cur_ref[0].astype(F32)
    row = i * tm + lax.broadcasted_iota(jnp.int32, (tm, 1), 0)
    local = lax.broadcasted_iota(jnp.int32, (tm, 1), 0)
    prev_row = prev_ref[0, 15:16, :].astype(F32)
    next_row = next_ref[0, 0:1, :].astype(F32)
    xm1 = jnp.where(local == 0, prev_row, pltpu.roll(x, 1, 0))
    xp1 = jnp.where(local == tm - 1, next_row, pltpu.roll(x, tm - 1, 0))
    xm1 = jnp.where((row == 0) | (row == t_len), 0.0, xm1)
    xp1 = jnp.where((row == t_len - 1) | (row == s_len - 1), 0.0, xp1)
    y = w_ref[0:1, :] * xm1 + w_ref[1:2, :] * x + w_ref[2:3, :] * xp1
    y = y * jax.nn.sigmoid(y)
    half = y.shape[1] // 2
    o_ref[0, :, :half] = (y[:, :half] * qscale).astype(BF16)
    o_ref[0, :, half:] = y[:, half:].astype(BF16)


def _conv_silu(qk_pre, conv_w, t_len):
    bsz, s_len, w = qk_pre.shape
    tm = ROW_TILE
    r16 = tm // 16
    n16 = s_len // 16
    kern = functools.partial(_conv_kernel, tm=tm, t_len=t_len, s_len=s_len, qscale=ML_DK ** -0.5)
    return pl.pallas_call(
        kern,
        grid=(bsz, s_len // tm),
        in_specs=[pl.BlockSpec((1, tm, w), lambda b, i: (b, i, 0)),
                  pl.BlockSpec((1, 16, w), lambda b, i: (b, jnp.maximum(i * r16 - 1, 0), 0)),
                  pl.BlockSpec((1, 16, w), lambda b, i: (b, jnp.minimum((i + 1) * r16, n16 - 1), 0)),
                  _const_spec((8, w))],
        out_specs=pl.BlockSpec((1, tm, w), lambda b, i: (b, i, 0)),
        out_shape=jax.ShapeDtypeStruct((bsz, s_len, w), BF16),
        compiler_params=_params(("parallel", "parallel")),
        name="conv_silu",
    )(qk_pre, qk_pre, qk_pre, jnp.pad(conv_w, ((0, 8 - conv_w.shape[0]), (0, 0))))


def _split3(a):
    a1 = a.astype(BF16)
    r1 = a - a1.astype(F32)
    a2 = r1.astype(BF16)
    a3 = (r1 - a2.astype(F32)).astype(BF16)
    return a1, a2, a3


def _mlstm_kernel(qkf_ref, qkb_ref, vf_ref, vb_ref, gf_ref, gb_ref, bias_ref,
                  hf_ref, hb_ref, c_sc, m_sc):
    L = ML_CHUNK

    @pl.when(pl.program_id(1) == 0)
    def _():
        c_sc[...] = jnp.zeros_like(c_sc)
        m_sc[...] = jnp.zeros_like(m_sc)

    r = lax.broadcasted_iota(jnp.int32, (L, L), 0)
    c = lax.broadcasted_iota(jnp.int32, (L, L), 1)
    lane = lax.broadcasted_iota(jnp.int32, (1, LANES), 1) - GATE_OFF
    is_forget = (lane >= 0) & (lane < N_GATES) & ((lane % (2 * ML_HEADS)) >= ML_HEADS)
    ones = jnp.ones((L, ML_DV), BF16)
    dirs = ((qkf_ref, vf_ref, gf_ref, hf_ref), (qkb_ref, vb_ref, gb_ref, hb_ref))
    for d, (qk_ref, v_ref, g_ref, h_ref) in enumerate(dirs):
        seen = (c <= r) if d == 0 else (c >= r)
        seen_b = seen.astype(F32).astype(BF16)
        last = L - 1 if d == 0 else 0
        gates = g_ref[0] + bias_ref[...]
        soft = jnp.minimum(gates, 0.0) - jnp.log1p(jnp.exp(-jnp.abs(gates)))
        logs = jnp.where(is_forget, soft, gates)
        logs_t = logs.T[GATE_OFF:GATE_OFF + N_GATES, :]
        cum_col = sum(jnp.dot(seen_b, p, preferred_element_type=F32) for p in _split3(logs))
        cum_row = sum(lax.dot_general(p, seen_b, (((1,), (1,)), ((), ())), preferred_element_type=F32)
                      for p in _split3(logs_t))
        for h in range(ML_HEADS):
            gi = d * 2 * ML_HEADS + h
            gf = gi + ML_HEADS
            b_col = cum_col[:, GATE_OFF + gf:GATE_OFF + gf + 1]
            b_row = cum_row[gf:gf + 1, :]
            li_col = logs[:, GATE_OFF + gi:GATE_OFF + gi + 1]
            li_row = logs_t[gi:gi + 1, :]
            b_end = b_row[:, last:last + 1]
            m_prev = m_sc[d, h][0:1, 0:1]
            sl = slice(h * LANES, (h + 1) * LANES)
            q = qk_ref[0, :, sl]
            k = qk_ref[0, :, ML_HEADS * ML_DK + h * LANES:ML_HEADS * ML_DK + (h + 1) * LANES]
            vext = jnp.concatenate([v_ref[0, :, sl], ones], axis=-1)
            state = c_sc[d, h]

            logw = jnp.where(seen, b_col - b_row + li_row, -jnp.inf)
            m_t = jnp.maximum(b_col + m_prev, jnp.max(logw, axis=-1, keepdims=True))
            w_state = jnp.exp(b_col + m_prev - m_t)
            qk = lax.dot_general(q, k, (((1,), (1,)), ((), ())), preferred_element_type=F32)
            s = qk * jnp.exp(logw - m_t)
            inter = jnp.dot(q, state.astype(BF16), preferred_element_type=F32)
            intra = jnp.dot(s.astype(BF16), vext, preferred_element_type=F32)
            mixed = w_state * inter + intra
            den = jnp.maximum(jnp.abs(mixed[:, ML_DV:]), jnp.exp(-m_t))
            h_ref[0, :, sl] = mixed[:, :ML_DV] / den

            g_col = b_end - b_col + li_col
            m_new = jnp.maximum(b_end + m_prev, jnp.max(g_col, axis=0, keepdims=True))
            decay = jnp.exp(b_end + m_prev - m_new)
            wk = (jnp.exp(g_col - m_new) * k.astype(F32)).astype(BF16)
            upd = lax.dot_general(wk, vext, (((0,), (0,)), ((), ())), preferred_element_type=F32)
            c_sc[d, h] = decay * state + upd
            m_sc[d, h] = jnp.broadcast_to(m_new, (8, LANES))


def _mlstm(qk, v, side, gate_bias_row, t_len):
    bsz, s_len, _ = qk.shape
    L = ML_CHUNK
    n_chunks = s_len // L
    n_lat = t_len // L
    hw = ML_HEADS * ML_DV
    fwd = lambda b, j: (b, (j + n_lat) % n_chunks, 0)
    bwd = lambda b, j: (b, n_chunks - 1 - j, 0)
    return pl.pallas_call(
        _mlstm_kernel,
        grid=(bsz, n_chunks),
        in_specs=[pl.BlockSpec((1, L, 2 * hw), fwd), pl.BlockSpec((1, L, 2 * hw), bwd),
                  pl.BlockSpec((1, L, hw), fwd), pl.BlockSpec((1, L, hw), bwd),
                  pl.BlockSpec((1, L, LANES), fwd), pl.BlockSpec((1, L, LANES), bwd),
                  _const_spec((1, LANES))],
        out_specs=[pl.BlockSpec((1, L, hw), fwd), pl.BlockSpec((1, L, hw), bwd)],
        out_shape=[jax.ShapeDtypeStruct((bsz, s_len, hw), F32)] * 2,
        scratch_shapes=[pltpu.VMEM((2, ML_HEADS, ML_DK, 2 * ML_DV), F32),
                        pltpu.VMEM((2, ML_HEADS, 8, LANES), F32)],
        compiler_params=_params(("parallel", "arbitrary")),
        name="mlstm",
    )(qk, qk, v, v, side, side, gate_bias_row)


def _finish_kernel(*refs, tm, t_len, with_readout, with_final, ff_chunk):
    it = iter(refs)
    h_ref = next(it)
    if with_readout:
        alat_ref, actx_ref, hf_ref, hb_ref, o_ref, hn_ref = (next(it) for _ in range(6))
    else:
        mix_ref = next(it)
    mx_ref, mc_ref, wo_ref, n2_ref, w1_ref, w2_ref = (next(it) for _ in range(6))
    fn_ref = next(it) if with_final else None
    out_ref = next(it)

    is_ctx = _row_is_ctx(pl.program_id(1), tm, t_len)
    mod = lambda j: jnp.where(is_ctx, mc_ref[j:j + 1, :], mx_ref[0, j:j + 1, :])

    if with_readout:
        hsum = hf_ref[0] + hb_ref[0]
        parts = [_rms(hsum[:, h * ML_DV:(h + 1) * ML_DV], hn_ref[:, h * ML_DV:(h + 1) * ML_DV])
                 for h in range(ML_HEADS)]
        rec = jax.nn.sigmoid(o_ref[0].astype(F32)) * jnp.concatenate(parts, axis=-1)
        att = jnp.where(is_ctx, actx_ref[0], alat_ref[0])
        mix = jnp.concatenate([att, rec.astype(BF16)], axis=-1)
    else:
        mix = mix_ref[0]
    s = h_ref[0] + mod(2) * jnp.dot(mix, wo_ref[...], preferred_element_type=F32)
    y = (_rms(s, n2_ref[...]) * (1.0 + mod(4)) + mod(3)).astype(BF16)
    d_ff = w1_ref.shape[1]
    acc = jnp.zeros(s.shape, F32)
    for c0 in range(0, d_ff, ff_chunk):
        u = jnp.maximum(jnp.dot(y, w1_ref[:, c0:c0 + ff_chunk], preferred_element_type=F32), 0.0)
        acc = acc + jnp.dot((u * u).astype(BF16), w2_ref[c0:c0 + ff_chunk, :],
                            preferred_element_type=F32)
    out = s + mod(5) * acc
    if with_final:
        out = _rms(out, fn_ref[...])
    out_ref[0] = out


def _finish(h, mix_parts, modsx, modsc, w_out, norm2, w1, w2, *, t_len, n_rows, head_norm=None,
            final_norm=None):
    bsz, _, d = h.shape
    tm = ROW_TILE
    with_readout = head_norm is not None
    with_final = final_norm is not None
    tok = lambda w: pl.BlockSpec((1, tm, w), lambda b, i: (b, i, 0))
    once = lambda shape: pl.BlockSpec(shape, lambda *_: (0,) * len(shape),
                                      pipeline_mode=pl.Buffered(1))
    args = [h] + list(mix_parts)
    specs = [tok(d)] + [tok(p.shape[-1]) for p in mix_parts]
    if with_readout:
        n_lat = t_len // tm
        aw = mix_parts[0].shape[-1]
        specs[1] = pl.BlockSpec((1, tm, aw), lambda b, i: (b, jnp.minimum(i, n_lat - 1), 0))
        specs[2] = pl.BlockSpec((1, tm, aw), lambda b, i: (b, jnp.maximum(i - n_lat, 0), 0))
        args.append(head_norm.reshape(1, -1))
        specs.append(_const_spec((1, head_norm.shape[0])))
    args += [modsx, modsc, w_out, norm2.reshape(1, d), w1, w2]
    specs += [pl.BlockSpec((1, 8, d), lambda b, i: (b, 0, 0)), _const_spec((8, d)),
              once(w_out.shape), _const_spec((1, d)), once(w1.shape), once(w2.shape)]
    if with_final:
        args.append(final_norm.reshape(1, d))
        specs.append(_const_spec((1, d)))
    kern = functools.partial(_finish_kernel, tm=tm, t_len=t_len, with_readout=with_readout,
                             with_final=with_final, ff_chunk=1024)
    return pl.pallas_call(
        kern,
        grid=(bsz, n_rows // tm),
        in_specs=specs,
        out_specs=tok(d),
        out_shape=jax.ShapeDtypeStruct((bsz, n_rows, d), F32),
        compiler_params=_params(("parallel", "parallel")),
        name="finish",
    )(*args)


def _grid_angles(n_tokens, d_rot):
    rows = n_tokens // GRID_W
    row, col = jnp.meshgrid(jnp.arange(rows), jnp.arange(GRID_W), indexing="ij")
    pos = jnp.stack([row.reshape(-1), col.reshape(-1)], axis=-1).astype(F32)
    n_freq = d_rot // 4
    freqs = ROPE_THETA ** (-jnp.arange(n_freq, dtype=F32) / n_freq)
    return (pos[:, :, None] * freqs).reshape(n_tokens, 2 * n_freq)


def _rope_tables(ang, lane0, ctx_len):
    t_len, n = ang.shape
    cos, sin = jnp.cos(ang), jnp.sin(ang)
    c = jnp.ones((t_len, LANES), F32).at[:, lane0:lane0 + n].set(cos).at[:, lane0 + n:lane0 + 2 * n].set(cos)
    s1 = jnp.zeros((t_len, LANES), F32).at[:, lane0 + n:lane0 + 2 * n].set(sin)
    s2 = jnp.zeros((t_len, LANES), F32).at[:, lane0:lane0 + n].set(-sin)
    pad = lambda a, v: jnp.concatenate([a, jnp.full((ctx_len, LANES), v, F32)], axis=0)
    return pad(c, 1.0), pad(s1, 0.0), pad(s2, 0.0)


def _mla_rope_cols():
    nf = MLA_ROPE // 4
    first = [a * 2 * nf + f for a in range(2) for f in range(nf)]
    second = [a * 2 * nf + nf + f for a in range(2) for f in range(nf)]
    return np.array(first + second)


def _gqa_head_cols():
    nf = GQA_DH // 4
    first = [a * 2 * nf + f for a in range(2) for f in range(nf)]
    second = [a * 2 * nf + nf + f for a in range(2) for f in range(nf)]
    return np.array(first + second)


def kernel(x, c, ctx, c_ctx,
           l0_ada_w, l0_ada_b, l0_norm1, l0_w_in, l0_mla_q_norm, l0_mla_w_uq, l0_mla_kv_norm, l0_mla_w_ukv,
           l0_ml_conv, l0_ml_gate_b, l0_ml_head_norm, l0_w_out, l0_norm2, l0_w1, l0_w2,
           l1_ada_w, l1_ada_b, l1_norm1, l1_w_in, l1_q_norm, l1_k_norm, l1_w_out, l1_norm2, l1_w1, l1_w2,
           final_norm):
    bsz, t_len, d = x.shape
    ctx_len = ctx.shape[1]
    h = jnp.concatenate([x, ctx], axis=1)

    cv = jnp.zeros((8, d), F32).at[:bsz].set(c).at[bsz].set(c_ctx)

    def mods(ada_w, ada_b):
        m = _adaln(cv, ada_w, ada_b).reshape(8, N_MOD, d)
        pad = jnp.zeros((8 - N_MOD, d), F32)
        modsx = jnp.concatenate([m[:bsz], jnp.broadcast_to(pad, (bsz,) + pad.shape)], axis=1)
        modsc = jnp.concatenate([m[bsz], pad], axis=0)
        return modsx, modsc

    modsx, modsc = mods(l0_ada_w, l0_ada_b)
    o = np.cumsum([0, MLA_Q_RANK, MLA_KV_RANK, MLA_ROPE, 2 * ML_HEADS * ML_DK, ML_HEADS * ML_DV,
                   ML_HEADS * ML_DV, N_GATES])
    w_cq, w_ckv, w_kpe, w_qk, w_v, w_o, w_g = (l0_w_in[:, o[i]:o[i + 1]] for i in range(7))
    side_pad = jnp.zeros((d, LANES - MLA_ROPE - N_GATES), F32)
    w_in0 = jnp.concatenate([w_qk, w_v, w_o, w_cq, w_ckv, w_kpe[:, _mla_rope_cols()], w_g, side_pad],
                            axis=1).astype(BF16)
    widths0 = (2 * ML_HEADS * ML_DK, ML_HEADS * ML_DV, ML_HEADS * ML_DV, MLA_Q_RANK + MLA_KV_RANK, LANES)
    qk_pre, v_ml, o_pre, lat, side = _inproj(h, l0_norm1, modsx, modsc, w_in0, widths0,
                                             (BF16, BF16, BF16, BF16, F32), t_len)

    dq = MLA_NOPE + MLA_ROPE
    head_cols = np.concatenate([np.arange(MLA_NOPE), MLA_NOPE + _mla_rope_cols()])
    wq = l0_mla_w_uq.reshape(MLA_Q_RANK, MLA_HEADS, dq)[:, :, head_cols]
    wq = jnp.pad(wq, ((0, 0), (0, 0), (0, LANES - dq))).reshape(MLA_Q_RANK, MLA_HEADS * LANES).astype(BF16)
    wkv = l0_mla_w_ukv.reshape(MLA_KV_RANK, MLA_HEADS, MLA_NOPE + MLA_V)
    wk_nope = jnp.pad(wkv[:, :, :MLA_NOPE], ((0, 0), (0, 0), (0, LANES - MLA_NOPE)))
    place = np.zeros((LANES, MLA_HEADS, LANES), np.float32)
    for hh in range(MLA_HEADS):
        place[KPE_OFF + np.arange(MLA_ROPE), hh, MLA_NOPE + np.arange(MLA_ROPE)] = 1.0
    wk = jnp.concatenate([wk_nope, jnp.asarray(place)], axis=0).reshape(
        MLA_KV_RANK + LANES, MLA_HEADS * LANES).astype(BF16)
    wv = wkv[:, :, MLA_NOPE:].reshape(MLA_KV_RANK, MLA_HEADS * MLA_V).astype(BF16)
    tabs_mla = _rope_tables(_grid_angles(t_len, MLA_ROPE), MLA_NOPE, ctx_len)
    q_mla, k_mla, v_mla = _mla_prep(lat, side, tabs_mla, l0_mla_q_norm, l0_mla_kv_norm, wq, wk, wv)
    mla_att = functools.partial(_attention, q_mla, k_mla, v_mla, n_groups=MLA_HEADS // 2, n_heads=2,
                                k_shared=False, v_pair=True, t_len=t_len)
    a_lat, a_ctx = mla_att(latent_queries=True), mla_att(latent_queries=False)

    qk = _conv_silu(qk_pre, l0_ml_conv, t_len)
    bias_row = jnp.zeros((1, LANES), F32).at[0, GATE_OFF:GATE_OFF + N_GATES].set(l0_ml_gate_b)
    hf, hb = _mlstm(qk, v_ml, side, bias_row, t_len)
    h = _finish(h, (a_lat, a_ctx, hf, hb, o_pre), modsx, modsc, l0_w_out.astype(BF16), l0_norm2,
                l0_w1.astype(BF16), l0_w2.astype(BF16), t_len=t_len, n_rows=t_len + ctx_len,
                head_norm=l0_ml_head_norm)

    modsx, modsc = mods(l1_ada_w, l1_ada_b)
    hc = _gqa_head_cols()
    n_qk = GQA_HEADS + GQA_KV_HEADS
    qk_cols = (np.arange(n_qk)[:, None] * GQA_DH + hc[None, :]).reshape(-1)
    w_in1 = jnp.concatenate([l1_w_in[:, qk_cols], l1_w_in[:, n_qk * GQA_DH:]], axis=1).astype(BF16)
    (hx,) = _inproj(h, l1_norm1, modsx, modsc, w_in1, (w_in1.shape[1],), (BF16,), t_len)
    tabs_gqa = _rope_tables(_grid_angles(t_len, GQA_DH), 0, ctx_len)
    q_gqa, k_gqa, v_gqa = _gqa_prep(hx, tabs_gqa, l1_q_norm[hc], l1_k_norm[hc])
    att = _attention(q_gqa, k_gqa, v_gqa, n_groups=GQA_KV_HEADS, n_heads=GQA_GROUP, k_shared=True,
                     v_pair=False, t_len=t_len, latent_queries=True)
    return _finish(h, (att,), modsx, modsc, l1_w_out.astype(BF16), l1_norm2,
                   l1_w1.astype(BF16), l1_w2.astype(BF16), t_len=t_len, n_rows=t_len,
                   final_norm=final_norm)
```

```python
import functools

import numpy as np
import jax
import jax.numpy as jnp
from jax import lax
from jax.experimental import pallas as pl
from jax.experimental.pallas import tpu as pltpu

F32 = jnp.float32
BF16 = jnp.bfloat16

GRID_W = 64
NORM_EPS = 1e-6
ROPE_THETA = 10000.0
N_MOD = 6
MLA_HEADS = 8
MLA_Q_RANK = 256
MLA_KV_RANK = 128
MLA_NOPE = 64
MLA_ROPE = 32
MLA_V = 64
ML_HEADS = 4
ML_DK = 128
ML_DV = 128
ML_CHUNK = 128
N_GATES = 4 * ML_HEADS
GQA_HEADS = 8
GQA_KV_HEADS = 2
GQA_GROUP = GQA_HEADS // GQA_KV_HEADS
GQA_DH = 128

LANES = 128
V7X_VMEM_BYTES = 64 * 1024 * 1024
VMEM_LIMIT = V7X_VMEM_BYTES * 7 // 8

ROW_TILE = 256
ATT_TQ = 512
ATT_TK = 512
ATT_UNROLL = 4
KPE_OFF = 0
GATE_OFF = 32


def _params(sem):
    return pltpu.CompilerParams(dimension_semantics=sem, vmem_limit_bytes=VMEM_LIMIT)


def _rms(x, g):
    var = jnp.mean(x * x, axis=-1, keepdims=True)
    return x * lax.rsqrt(var + NORM_EPS) * g


def _const_spec(shape):
    nd = len(shape)
    return pl.BlockSpec(shape, lambda *_: (0,) * nd)


def _adaln_kernel(c_ref, w_ref, b_ref, o_ref):
    c = c_ref[...]
    a = c * jax.nn.sigmoid(c)
    o_ref[...] = jnp.dot(a.astype(BF16), w_ref[...].astype(BF16),
                         preferred_element_type=F32) + b_ref[...]


def _adaln(cv, w, b):
    d = cv.shape[1]
    n = w.shape[1]
    tn = n // 4
    return pl.pallas_call(
        _adaln_kernel,
        grid=(n // tn,),
        in_specs=[pl.BlockSpec((8, d), lambda j: (0, 0)),
                  pl.BlockSpec((d, tn), lambda j: (0, j)),
                  pl.BlockSpec((1, tn), lambda j: (0, j))],
        out_specs=pl.BlockSpec((8, tn), lambda j: (0, j)),
        out_shape=jax.ShapeDtypeStruct((8, n), F32),
        compiler_params=_params(("arbitrary",)),
        name="adaln",
    )(cv, w, b.reshape(1, n))


def _row_is_ctx(tile, tm, t_len):
    row = tile * tm + lax.broadcasted_iota(jnp.int32, (tm, 1), 0)
    return row >= t_len


def _stream_specs(hs, tm, t_len):
    d = hs[0].shape[-1]
    if len(hs) == 1:
        return [pl.BlockSpec((1, tm, d), lambda b, i: (b, i, 0))]
    n_lat = t_len // tm
    return [pl.BlockSpec((1, tm, d), lambda b, i: (b, jnp.minimum(i, n_lat - 1), 0)),
            pl.BlockSpec((1, tm, d), lambda b, i: (b, jnp.maximum(i - n_lat, 0), 0))]


def _stream_tile(h_refs, is_ctx):
    if len(h_refs) == 1:
        return h_refs[0][0]
    return jnp.where(is_ctx, h_refs[1][0], h_refs[0][0])


def _inproj_kernel(*refs, n_stream, tm, t_len, widths):
    h_refs, (g_ref, mx_ref, mc_ref, w_ref), out_refs = (
        refs[:n_stream], refs[n_stream:n_stream + 4], refs[n_stream + 4:])
    is_ctx = _row_is_ctx(pl.program_id(1), tm, t_len)
    shift = jnp.where(is_ctx, mc_ref[0:1, :], mx_ref[0, 0:1, :])
    scale = jnp.where(is_ctx, mc_ref[1:2, :], mx_ref[0, 1:2, :])
    y = (_rms(_stream_tile(h_refs, is_ctx), g_ref[...]) * (1.0 + scale) + shift).astype(BF16)
    off = 0
    for o_ref, width in zip(out_refs, widths):
        o_ref[0] = jnp.dot(y, w_ref[:, off:off + width],
                           preferred_element_type=F32).astype(o_ref.dtype)
        off += width


def _inproj(hs, g, modsx, modsc, w, widths, dtypes, t_len, s_len):
    bsz, _, d = hs[0].shape
    tm = ROW_TILE
    kern = functools.partial(_inproj_kernel, n_stream=len(hs), tm=tm, t_len=t_len, widths=widths)
    return pl.pallas_call(
        kern,
        grid=(bsz, s_len // tm),
        in_specs=_stream_specs(hs, tm, t_len) + [
            _const_spec((1, d)),
            pl.BlockSpec((1, 8, d), lambda b, i: (b, 0, 0)),
            _const_spec((8, d)),
            _const_spec(w.shape)],
        out_specs=[pl.BlockSpec((1, tm, wd), lambda b, i: (b, i, 0)) for wd in widths],
        out_shape=[jax.ShapeDtypeStruct((bsz, s_len, wd), dt) for wd, dt in zip(widths, dtypes)],
        compiler_params=_params(("parallel", "parallel")),
        name="inproj",
    )(*hs, g.reshape(1, d), modsx, modsc, w)


def _rope(x, c, s1, s2, half):
    return x * c + pltpu.roll(x, half, 1) * s1 + pltpu.roll(x, LANES - half, 1) * s2


LOG2E = 1.4426950408889634
BF16_SUBLANES = 16
VT_ROWS = LANES + BF16_SUBLANES


def _store_vt(vt_ref, vt, tm):
    for g in range(vt.shape[0] // LANES):
        vt_ref[0, g * VT_ROWS:g * VT_ROWS + LANES, :] = vt[g * LANES:(g + 1) * LANES].astype(BF16)
        vt_ref[0, g * VT_ROWS + LANES:(g + 1) * VT_ROWS, :] = jnp.ones((BF16_SUBLANES, tm), BF16)


def _mla_prep_kernel(lat_ref, side_ref, c_ref, s1_ref, s2_ref, qn_ref, kvn_ref,
                     wq_ref, wk_ref, wv_ref, qt_ref, k_ref, vt_ref, *, scale):
    lat = lat_ref[0].astype(F32)
    cqn = _rms(lat[:, :MLA_Q_RANK], qn_ref[...]).astype(BF16)
    ckvn = _rms(lat[:, MLA_Q_RANK:], kvn_ref[...]).astype(BF16)
    c, s1, s2 = c_ref[...], s1_ref[...], s2_ref[...]
    q = jnp.dot(cqn, wq_ref[...], preferred_element_type=F32)
    kin = jnp.concatenate([ckvn, side_ref[0].astype(BF16)], axis=-1)
    k = jnp.dot(kin, wk_ref[...], preferred_element_type=F32)
    half = MLA_ROPE // 2
    for h in range(MLA_HEADS):
        sl = slice(h * LANES, (h + 1) * LANES)
        qt_ref[0, sl, :] = (_rope(q[:, sl], c, s1, s2, half) * scale).T.astype(BF16)
        k_ref[0, :, sl] = _rope(k[:, sl], c, s1, s2, half).astype(BF16)
    _store_vt(vt_ref, jnp.dot(ckvn, wv_ref[...], preferred_element_type=F32).T, lat.shape[0])


def _mla_prep(lat, side, tabs, qn, kvn, wq, wk, wv):
    bsz, s_len, lw = lat.shape
    tm = ROW_TILE
    hw = MLA_HEADS * LANES
    vw = MLA_HEADS * MLA_V
    tok = lambda w: pl.BlockSpec((1, tm, w), lambda b, i: (b, i, 0))
    tok_t = lambda w: pl.BlockSpec((1, w, tm), lambda b, i: (b, 0, i))
    tab = pl.BlockSpec((tm, LANES), lambda b, i: (i, 0))
    kern = functools.partial(_mla_prep_kernel, scale=LOG2E * (MLA_NOPE + MLA_ROPE) ** -0.5)
    return pl.pallas_call(
        kern,
        grid=(bsz, s_len // tm),
        in_specs=[tok(lw), tok(LANES), tab, tab, tab,
                  _const_spec((1, MLA_Q_RANK)), _const_spec((1, MLA_KV_RANK)),
                  _const_spec(wq.shape), _const_spec(wk.shape), _const_spec(wv.shape)],
        out_specs=[tok_t(hw), tok(hw), tok_t(vw // LANES * VT_ROWS)],
        out_shape=[jax.ShapeDtypeStruct((bsz, hw, s_len), BF16),
                   jax.ShapeDtypeStruct((bsz, s_len, hw), BF16),
                   jax.ShapeDtypeStruct((bsz, vw // LANES * VT_ROWS, s_len), BF16)],
        compiler_params=_params(("parallel", "parallel")),
        name="mla_prep",
    )(lat, side, *tabs, qn.reshape(1, -1), kvn.reshape(1, -1), wq, wk, wv)


def _gqa_prep_kernel(q_in, k_in, v_in, c_ref, s1_ref, s2_ref, qn_ref, kn_ref,
                     qt_ref, k_ref, vt_ref, *, scale):
    c, s1, s2 = c_ref[...], s1_ref[...], s2_ref[...]
    half = GQA_DH // 2
    for h in range(GQA_HEADS):
        sl = slice(h * LANES, (h + 1) * LANES)
        x = _rms(q_in[0, :, sl].astype(F32), qn_ref[...])
        qt_ref[0, sl, :] = (_rope(x, c, s1, s2, half) * scale).T.astype(BF16)
    for h in range(GQA_KV_HEADS):
        sl = slice(h * LANES, (h + 1) * LANES)
        x = _rms(k_in[0, :, sl].astype(F32), kn_ref[...])
        k_ref[0, :, sl] = _rope(x, c, s1, s2, half).astype(BF16)
    _store_vt(vt_ref, v_in[0].astype(F32).T, v_in.shape[1])


def _gqa_prep(hx, tabs, qn, kn):
    bsz, s_len, _ = hx.shape
    tm = ROW_TILE
    qw = GQA_HEADS * GQA_DH
    kw = GQA_KV_HEADS * GQA_DH
    tab = pl.BlockSpec((tm, LANES), lambda b, i: (i, 0))
    kern = functools.partial(_gqa_prep_kernel, scale=LOG2E * GQA_DH ** -0.5)
    return pl.pallas_call(
        kern,
        grid=(bsz, s_len // tm),
        in_specs=[pl.BlockSpec((1, tm, qw), lambda b, i: (b, i, 0)),
                  pl.BlockSpec((1, tm, kw), lambda b, i: (b, i, qw // kw)),
                  pl.BlockSpec((1, tm, kw), lambda b, i: (b, i, qw // kw + 1)),
                  tab, tab, tab, _const_spec((1, GQA_DH)), _const_spec((1, GQA_DH))],
        out_specs=[pl.BlockSpec((1, qw, tm), lambda b, i: (b, 0, i)),
                   pl.BlockSpec((1, tm, kw), lambda b, i: (b, i, 0)),
                   pl.BlockSpec((1, kw // LANES * VT_ROWS, tm), lambda b, i: (b, 0, i))],
        out_shape=[jax.ShapeDtypeStruct((bsz, qw, s_len), BF16),
                   jax.ShapeDtypeStruct((bsz, s_len, kw), BF16),
                   jax.ShapeDtypeStruct((bsz, kw // LANES * VT_ROWS, s_len), BF16)],
        compiler_params=_params(("parallel", "parallel")),
        name="gqa_prep",
    )(hx, hx, hx, *tabs, qn.reshape(1, -1), kn.reshape(1, -1))


def _attn_kernel(qt_ref, k_ref, vt_ref, o_ref, m_sc, l_sc, acc_sc, *,
                 n_heads, k_shared, v_pair, latent_queries, tq, tk, t_len, s_len):
    def scores(g, kc):
        kg = kc if k_shared else kc[:, g * LANES:(g + 1) * LANES]
        return jnp.dot(kg, qt_ref[0, g * LANES:(g + 1) * LANES, :], preferred_element_type=F32)

    def k_chunk(j):
        return k_ref[0, pl.ds(pl.multiple_of(j * tk, tk), tk), :]

    def update(g, st, vte, first):
        m_new = jnp.max(st, axis=0, keepdims=True)
        if not first:
            m_old = m_sc[g][0:1]
            m_new = jnp.maximum(m_old, m_new)
            alpha = jnp.exp2(m_old - m_new)
        p = jnp.exp2(st - m_new).astype(BF16)
        res = jnp.dot(vte, p, preferred_element_type=F32)
        if first:
            acc_sc[g] = res[:LANES]
            l_sc[g] = res[LANES:LANES + 8]
        else:
            acc_sc[g] = alpha * acc_sc[g] + res[:LANES]
            l_sc[g] = alpha * l_sc[g] + res[LANES:LANES + 8]
        m_sc[g] = jnp.broadcast_to(m_new, (8, tq))

    kc = k_ref[0, t_len:s_len, :]
    vte = vt_ref[0, :, t_len:s_len]
    st_next = scores(0, kc)
    for g in range(n_heads):
        st = st_next
        if g + 1 < n_heads:
            st_next = scores(g + 1, kc)
        elif latent_queries:
            st_next = scores(0, k_chunk(0))
        update(g, st, vte, True)

    if latent_queries:
        n_chunks = t_len // tk

        def body(j, st_next):
            kc = k_chunk(j)
            vte = vt_ref[0, :, pl.ds(pl.multiple_of(j * tk, tk), tk)]
            for g in range(n_heads):
                st = st_next
                if g + 1 < n_heads:
                    st_next = scores(g + 1, kc)
                else:
                    st_next = scores(0, k_chunk(jnp.minimum(j + 1, n_chunks - 1)))
                update(g, st, vte, False)
            return st_next

        lax.fori_loop(0, n_chunks, body, st_next, unroll=ATT_UNROLL)

    outs = [acc_sc[g] / l_sc[g][0:1] for g in range(n_heads)]
    if v_pair:
        row = lax.broadcasted_iota(jnp.int32, (LANES, tq), 0)
        o_ref[0] = jnp.where(row < LANES // 2, outs[0], outs[1]).T.astype(o_ref.dtype)
    else:
        for g in range(n_heads):
            o_ref[0, :, g * LANES:(g + 1) * LANES] = outs[g].T.astype(o_ref.dtype)


def _attention(qt, k, vt, *, n_groups, n_heads, k_shared, v_pair, t_len, latent_queries):
    bsz, s_len, _ = k.shape
    tq = ATT_TQ if latent_queries else s_len - t_len
    n_q_rows = t_len if latent_queries else s_len - t_len
    q0 = 0 if latent_queries else t_len // tq
    tk = ATT_TK
    qw = n_heads * LANES
    kw = LANES if k_shared else qw
    ow = LANES if v_pair else qw
    kern = functools.partial(_attn_kernel, n_heads=n_heads, k_shared=k_shared, v_pair=v_pair,
                             latent_queries=latent_queries, tq=tq, tk=tk, t_len=t_len, s_len=s_len)
    return pl.pallas_call(
        kern,
        grid=(bsz, n_groups, n_q_rows // tq),
        in_specs=[pl.BlockSpec((1, qw, tq), lambda b, g, i: (b, g, q0 + i)),
                  pl.BlockSpec((1, s_len, kw), lambda b, g, i: (b, 0, g)),
                  pl.BlockSpec((1, VT_ROWS, s_len), lambda b, g, i: (b, g, 0))],
        out_specs=pl.BlockSpec((1, tq, ow), lambda b, g, i: (b, i, g)),
        out_shape=jax.ShapeDtypeStruct((bsz, n_q_rows, n_groups * ow), BF16),
        scratch_shapes=[pltpu.VMEM((n_heads, 8, tq), F32),
                        pltpu.VMEM((n_heads, 8, tq), F32),
                        pltpu.VMEM((n_heads, LANES, tq), F32)],
        compiler_params=_params(("parallel", "parallel", "arbitrary")),
        name="attention",
    )(qt, k, vt)


def _conv_kernel(cur_ref, prev_ref, next_ref, w_ref, o_ref, *, tm, t_len, s_len, qscale):
    i = pl.program_id(1)
    x = cur_ref[0].astype(F32)
    row = i * tm + lax.broadcasted_iota(jnp.int32, (tm, 1), 0)
    local = lax.broadcasted_iota(jnp.int32, (tm, 1), 0)
    prev_row = prev_ref[0, 15:16, :].astype(F32)
    next_row = next_ref[0, 0:1, :].astype(F32)
    xm1 = jnp.where(local == 0, prev_row, pltpu.roll(x, 1, 0))
    xp1 = jnp.where(local == tm - 1, next_row, pltpu.roll(x, tm - 1, 0))
    xm1 = jnp.where((row == 0) | (row == t_len), 0.0, xm1)
    xp1 = jnp.where((row == t_len - 1) | (row == s_len - 1), 0.0, xp1)
    y = w_ref[0:1, :] * xm1 + w_ref[1:2, :] * x + w_ref[2:3, :] * xp1
    y = y * jax.nn.sigmoid(y)
    half = y.shape[1] // 2
    o_ref[0, :, :half] = (y[:, :half] * qscale).astype(BF16)
    o_ref[0, :, half:] = y[:, half:].astype(BF16)


def _conv_silu(qk_pre, conv_w, t_len):
    bsz, s_len, w = qk_pre.shape
    tm = ROW_TILE
    r16 = tm // 16
    n16 = s_len // 16
    kern = functools.partial(_conv_kernel, tm=tm, t_len=t_len, s_len=s_len, qscale=ML_DK ** -0.5)
    return pl.pallas_call(
        kern,
        grid=(bsz, s_len // tm),
        in_specs=[pl.BlockSpec((1, tm, w), lambda b, i: (b, i, 0)),
                  pl.BlockSpec((1, 16, w), lambda b, i: (b, jnp.maximum(i * r16 - 1, 0), 0)),
                  pl.BlockSpec((1, 16, w), lambda b, i: (b, jnp.minimum((i + 1) * r16, n16 - 1), 0)),
                  _const_spec((8, w))],
        out_specs=pl.BlockSpec((1, tm, w), lambda b, i: (b, i, 0)),
        out_shape=jax.ShapeDtypeStruct((bsz, s_len, w), BF16),
        compiler_params=_params(("parallel", "parallel")),
        name="conv_silu",
    )(qk_pre, qk_pre, qk_pre, jnp.pad(conv_w, ((0, 8 - conv_w.shape[0]), (0, 0))))


def _split3(a):
    a1 = a.astype(BF16)
    r1 = a - a1.astype(F32)
    a2 = r1.astype(BF16)
    a3 = (r1 - a2.astype(F32)).astype(BF16)
    return a1, a2, a3


def _mlstm_kernel(qkf_ref, qkb_ref, vf_ref, vb_ref, gf_ref, gb_ref, bias_ref,
                  hf_ref, hb_ref, c_sc, m_sc):
    L = ML_CHUNK

    @pl.when(pl.program_id(1) == 0)
    def _():
        c_sc[...] = jnp.zeros_like(c_sc)
        m_sc[...] = jnp.zeros_like(m_sc)

    r = lax.broadcasted_iota(jnp.int32, (L, L), 0)
    c = lax.broadcasted_iota(jnp.int32, (L, L), 1)
    lane = lax.broadcasted_iota(jnp.int32, (1, LANES), 1) - GATE_OFF
    is_forget = (lane >= 0) & (lane < N_GATES) & ((lane % (2 * ML_HEADS)) >= ML_HEADS)
    ones = jnp.ones((L, ML_DV), BF16)
    dirs = ((qkf_ref, vf_ref, gf_ref, hf_ref), (qkb_ref, vb_ref, gb_ref, hb_ref))
    for d, (qk_ref, v_ref, g_ref, h_ref) in enumerate(dirs):
        seen = (c <= r) if d == 0 else (c >= r)
        seen_b = seen.astype(F32).astype(BF16)
        last = L - 1 if d == 0 else 0
        gates = g_ref[0] + bias_ref[...]
        soft = jnp.minimum(gates, 0.0) - jnp.log1p(jnp.exp(-jnp.abs(gates)))
        logs = jnp.where(is_forget, soft, gates)
        logs_t = logs.T[GATE_OFF:GATE_OFF + N_GATES, :]
        cum_col = sum(jnp.dot(seen_b, p, preferred_element_type=F32) for p in _split3(logs))
        cum_row = sum(lax.dot_general(p, seen_b, (((1,), (1,)), ((), ())), preferred_element_type=F32)
                      for p in _split3(logs_t))
        for h in range(ML_HEADS):
            gi = d * 2 * ML_HEADS + h
            gf = gi + ML_HEADS
            b_col = cum_col[:, GATE_OFF + gf:GATE_OFF + gf + 1]
            b_row = cum_row[gf:gf + 1, :]
            li_col = logs[:, GATE_OFF + gi:GATE_OFF + gi + 1]
            li_row = logs_t[gi:gi + 1, :]
            b_end = b_row[:, last:last + 1]
            m_prev = m_sc[d, h][0:1, 0:1]
            sl = slice(h * LANES, (h + 1) * LANES)
            q = qk_ref[0, :, sl]
            k = qk_ref[0, :, ML_HEADS * ML_DK + h * LANES:ML_HEADS * ML_DK + (h + 1) * LANES]
            vext = jnp.concatenate([v_ref[0, :, sl], ones], axis=-1)
            state = c_sc[d, h]

            logw = jnp.where(seen, b_col - b_row + li_row, -jnp.inf)
            m_t = jnp.maximum(b_col + m_prev, jnp.max(logw, axis=-1, keepdims=True))
            w_state = jnp.exp(b_col + m_prev - m_t)
            qk = lax.dot_general(q, k, (((1,), (1,)), ((), ())), preferred_element_type=F32)
            s = qk * jnp.exp(logw - m_t)
            inter = jnp.dot(q, state.astype(BF16), preferred_element_type=F32)
            intra = jnp.dot(s.astype(BF16), vext, preferred_element_type=F32)
            mixed = w_state * inter + intra
            den = jnp.maximum(jnp.abs(mixed[:, ML_DV:]), jnp.exp(-m_t))
            h_ref[0, :, sl] = mixed[:, :ML_DV] / den

            g_col = b_end - b_col + li_col
            m_new = jnp.maximum(b_end + m_prev, jnp.max(g_col, axis=0, keepdims=True))
            decay = jnp.exp(b_end + m_prev - m_new)
            wk = (jnp.exp(g_col - m_new) * k.astype(F32)).astype(BF16)
            upd = lax.dot_general(wk, vext, (((0,), (0,)), ((), ())), preferred_element_type=F32)
            c_sc[d, h] = decay * state + upd
            m_sc[d, h] = jnp.broadcast_to(m_new, (8, LANES))


def _mlstm(qk, v, side, gate_bias_row, t_len):
    bsz, s_len, _ = qk.shape
    L = ML_CHUNK
    n_chunks = s_len // L
    n_lat = t_len // L
    hw = ML_HEADS * ML_DV
    fwd = lambda b, j: (b, (j + n_lat) % n_chunks, 0)
    bwd = lambda b, j: (b, n_chunks - 1 - j, 0)
    return pl.pallas_call(
        _mlstm_kernel,
        grid=(bsz, n_chunks),
        in_specs=[pl.BlockSpec((1, L, 2 * hw), fwd), pl.BlockSpec((1, L, 2 * hw), bwd),
                  pl.BlockSpec((1, L, hw), fwd), pl.BlockSpec((1, L, hw), bwd),
                  pl.BlockSpec((1, L, LANES), fwd), pl.BlockSpec((1, L, LANES), bwd),
                  _const_spec((1, LANES))],
        out_specs=[pl.BlockSpec((1, L, hw), fwd), pl.BlockSpec((1, L, hw), bwd)],
        out_shape=[jax.ShapeDtypeStruct((bsz, s_len, hw), F32)] * 2,
        scratch_shapes=[pltpu.VMEM((2, ML_HEADS, ML_DK, 2 * ML_DV), F32),
                        pltpu.VMEM((2, ML_HEADS, 8, LANES), F32)],
        compiler_params=_params(("parallel", "arbitrary")),
        name="mlstm",
    )(qk, qk, v, v, side, side, gate_bias_row)


def _finish_kernel(*refs, n_stream, tm, t_len, with_readout, with_final, ff_chunk):
    h_refs = refs[:n_stream]
    it = iter(refs[n_stream:])
    if with_readout:
        alat_ref, actx_ref, hf_ref, hb_ref, o_ref, hn_ref = (next(it) for _ in range(6))
    else:
        mix_ref = next(it)
    mx_ref, mc_ref, wo_ref, n2_ref, w1_ref, w2_ref = (next(it) for _ in range(6))
    fn_ref = next(it) if with_final else None
    out_ref = next(it)

    is_ctx = _row_is_ctx(pl.program_id(1), tm, t_len)
    mod = lambda j: jnp.where(is_ctx, mc_ref[j:j + 1, :], mx_ref[0, j:j + 1, :])

    if with_readout:
        hsum = hf_ref[0] + hb_ref[0]
        parts = [_rms(hsum[:, h * ML_DV:(h + 1) * ML_DV], hn_ref[:, h * ML_DV:(h + 1) * ML_DV])
                 for h in range(ML_HEADS)]
        rec = jax.nn.sigmoid(o_ref[0].astype(F32)) * jnp.concatenate(parts, axis=-1)
        att = jnp.where(is_ctx, actx_ref[0], alat_ref[0])
        mix = jnp.concatenate([att, rec.astype(BF16)], axis=-1)
    else:
        mix = mix_ref[0]
    s = _stream_tile(h_refs, is_ctx) + mod(2) * jnp.dot(mix, wo_ref[...], preferred_element_type=F32)
    y = (_rms(s, n2_ref[...]) * (1.0 + mod(4)) + mod(3)).astype(BF16)
    d_ff = w1_ref.shape[1]
    acc = jnp.zeros(s.shape, F32)
    for c0 in range(0, d_ff, ff_chunk):
        u = jnp.maximum(jnp.dot(y, w1_ref[:, c0:c0 + ff_chunk], preferred_element_type=F32), 0.0)
        acc = acc + jnp.dot((u * u).astype(BF16), w2_ref[c0:c0 + ff_chunk, :],
                            preferred_element_type=F32)
    out = s + mod(5) * acc
    if with_final:
        out = _rms(out, fn_ref[...])
    out_ref[0] = out


def _finish(hs, mix_parts, modsx, modsc, w_out, norm2, w1, w2, *, t_len, n_rows, head_norm=None,
            final_norm=None):
    bsz, _, d = hs[0].shape
    tm = ROW_TILE
    ns = len(hs)
    with_readout = head_norm is not None
    with_final = final_norm is not None
    tok = lambda w: pl.BlockSpec((1, tm, w), lambda b, i: (b, i, 0))
    once = lambda shape: pl.BlockSpec(shape, lambda *_: (0,) * len(shape),
                                      pipeline_mode=pl.Buffered(1))
    args = list(hs) + list(mix_parts)
    specs = _stream_specs(hs, tm, t_len) + [tok(p.shape[-1]) for p in mix_parts]
    if with_readout:
        n_lat = t_len // tm
        aw = mix_parts[0].shape[-1]
        specs[ns] =pl.BlockSpec((1, tm, aw), lambda b, i: (b, jnp.minimum(i, n_lat - 1), 0))
        specs[ns + 1] =pl.BlockSpec((1, tm, aw), lambda b, i: (b, jnp.maximum(i - n_lat, 0), 0))
        args.append(head_norm.reshape(1, -1))
        specs.append(_const_spec((1, head_norm.shape[0])))
    args += [modsx, modsc, w_out, norm2.reshape(1, d), w1, w2]
    specs += [pl.BlockSpec((1, 8, d), lambda b, i: (b, 0, 0)), _const_spec((8, d)),
              once(w_out.shape), _const_spec((1, d)), once(w1.shape), once(w2.shape)]
    if with_final:
        args.append(final_norm.reshape(1, d))
        specs.append(_const_spec((1, d)))
    kern = functools.partial(_finish_kernel, n_stream=ns, tm=tm, t_len=t_len, with_readout=with_readout,
                             with_final=with_final, ff_chunk=1024)
    return pl.pallas_call(
        kern,
        grid=(bsz, n_rows // tm),
        in_specs=specs,
        out_specs=tok(d),
        out_shape=jax.ShapeDtypeStruct((bsz, n_rows, d), F32),
        compiler_params=_params(("parallel", "parallel")),
        name="finish",
    )(*args)


def _grid_angles(n_tokens, d_rot):
    rows = n_tokens // GRID_W
    row, col = jnp.meshgrid(jnp.arange(rows), jnp.arange(GRID_W), indexing="ij")
    pos = jnp.stack([row.reshape(-1), col.reshape(-1)], axis=-1).astype(F32)
    n_freq = d_rot // 4
    freqs = ROPE_THETA ** (-jnp.arange(n_freq, dtype=F32) / n_freq)
    return (pos[:, :, None] * freqs).reshape(n_tokens, 2 * n_freq)


def _rope_tables(ang, lane0, ctx_len):
    t_len, n = ang.shape
    cos, sin = jnp.cos(ang), jnp.sin(ang)
    c = jnp.ones((t_len, LANES), F32).at[:, lane0:lane0 + n].set(cos).at[:, lane0 + n:lane0 + 2 * n].set(cos)
    s1 = jnp.zeros((t_len, LANES), F32).at[:, lane0 + n:lane0 + 2 * n].set(sin)
    s2 = jnp.zeros((t_len, LANES), F32).at[:, lane0:lane0 + n].set(-sin)
    pad = lambda a, v: jnp.concatenate([a, jnp.full((ctx_len, LANES), v, F32)], axis=0)
    return pad(c, 1.0), pad(s1, 0.0), pad(s2, 0.0)


def _mla_rope_cols():
    nf = MLA_ROPE // 4
    first = [a * 2 * nf + f for a in range(2) for f in range(nf)]
    second = [a * 2 * nf + nf + f for a in range(2) for f in range(nf)]
    return np.array(first + second)


def _gqa_head_cols():
    nf = GQA_DH // 4
    first = [a * 2 * nf + f for a in range(2) for f in range(nf)]
    second = [a * 2 * nf + nf + f for a in range(2) for f in range(nf)]
    return np.array(first + second)


def kernel(x, c, ctx, c_ctx,
           l0_ada_w, l0_ada_b, l0_norm1, l0_w_in, l0_mla_q_norm, l0_mla_w_uq, l0_mla_kv_norm, l0_mla_w_ukv,
           l0_ml_conv, l0_ml_gate_b, l0_ml_head_norm, l0_w_out, l0_norm2, l0_w1, l0_w2,
           l1_ada_w, l1_ada_b, l1_norm1, l1_w_in, l1_q_norm, l1_k_norm, l1_w_out, l1_norm2, l1_w1, l1_w2,
           final_norm):
    bsz, t_len, d = x.shape
    ctx_len = ctx.shape[1]
    s_len = t_len + ctx_len

    cv = jnp.zeros((8, d), F32).at[:bsz].set(c).at[bsz].set(c_ctx)

    def mods(ada_w, ada_b):
        m = _adaln(cv, ada_w, ada_b).reshape(8, N_MOD, d)
        pad = jnp.zeros((8 - N_MOD, d), F32)
        modsx = jnp.concatenate([m[:bsz], jnp.broadcast_to(pad, (bsz,) + pad.shape)], axis=1)
        modsc = jnp.concatenate([m[bsz], pad], axis=0)
        return modsx, modsc

    modsx, modsc = mods(l0_ada_w, l0_ada_b)
    o = np.cumsum([0, MLA_Q_RANK, MLA_KV_RANK, MLA_ROPE, 2 * ML_HEADS * ML_DK, ML_HEADS * ML_DV,
                   ML_HEADS * ML_DV, N_GATES])
    w_cq, w_ckv, w_kpe, w_qk, w_v, w_o, w_g = (l0_w_in[:, o[i]:o[i + 1]] for i in range(7))
    side_pad = jnp.zeros((d, LANES - MLA_ROPE - N_GATES), F32)
    w_in0 = jnp.concatenate([w_qk, w_v, w_o, w_cq, w_ckv, w_kpe[:, _mla_rope_cols()], w_g, side_pad],
                            axis=1).astype(BF16)
    widths0 = (2 * ML_HEADS * ML_DK, ML_HEADS * ML_DV, ML_HEADS * ML_DV, MLA_Q_RANK + MLA_KV_RANK, LANES)
    qk_pre, v_ml, o_pre, lat, side = _inproj((x, ctx), l0_norm1, modsx, modsc, w_in0, widths0,
                                             (BF16, BF16, BF16, BF16, F32), t_len, s_len)

    dq = MLA_NOPE + MLA_ROPE
    head_cols = np.concatenate([np.arange(MLA_NOPE), MLA_NOPE + _mla_rope_cols()])
    wq = l0_mla_w_uq.reshape(MLA_Q_RANK, MLA_HEADS, dq)[:, :, head_cols]
    wq = jnp.pad(wq, ((0, 0), (0, 0), (0, LANES - dq))).reshape(MLA_Q_RANK, MLA_HEADS * LANES).astype(BF16)
    wkv = l0_mla_w_ukv.reshape(MLA_KV_RANK, MLA_HEADS, MLA_NOPE + MLA_V)
    wk_nope = jnp.pad(wkv[:, :, :MLA_NOPE], ((0, 0), (0, 0), (0, LANES - MLA_NOPE)))
    place = np.zeros((LANES, MLA_HEADS, LANES), np.float32)
    for hh in range(MLA_HEADS):
        place[KPE_OFF + np.arange(MLA_ROPE), hh, MLA_NOPE + np.arange(MLA_ROPE)] = 1.0
    wk = jnp.concatenate([wk_nope, jnp.asarray(place)], axis=0).reshape(
        MLA_KV_RANK + LANES, MLA_HEADS * LANES).astype(BF16)
    wv = wkv[:, :, MLA_NOPE:].reshape(MLA_KV_RANK, MLA_HEADS * MLA_V).astype(BF16)
    tabs_mla = _rope_tables(_grid_angles(t_len, MLA_ROPE), MLA_NOPE, ctx_len)
    q_mla, k_mla, v_mla = _mla_prep(lat, side, tabs_mla, l0_mla_q_norm, l0_mla_kv_norm, wq, wk, wv)
    mla_att = functools.partial(_attention, q_mla, k_mla, v_mla, n_groups=MLA_HEADS // 2, n_heads=2,
                                k_shared=False, v_pair=True, t_len=t_len)
    a_lat, a_ctx = mla_att(latent_queries=True), mla_att(latent_queries=False)

    qk = _conv_silu(qk_pre, l0_ml_conv, t_len)
    bias_row = jnp.zeros((1, LANES), F32).at[0, GATE_OFF:GATE_OFF + N_GATES].set(l0_ml_gate_b)
    hf, hb = _mlstm(qk, v_ml, side, bias_row, t_len)
    h = _finish((x, ctx), (a_lat, a_ctx, hf, hb, o_pre), modsx, modsc, l0_w_out.astype(BF16),
                l0_norm2, l0_w1.astype(BF16), l0_w2.astype(BF16), t_len=t_len, n_rows=s_len,
                head_norm=l0_ml_head_norm)

    modsx, modsc = mods(l1_ada_w, l1_ada_b)
    hc = _gqa_head_cols()
    n_qk = GQA_HEADS + GQA_KV_HEADS
    qk_cols = (np.arange(n_qk)[:, None] * GQA_DH + hc[None, :]).reshape(-1)
    w_in1 = jnp.concatenate([l1_w_in[:, qk_cols], l1_w_in[:, n_qk * GQA_DH:]], axis=1).astype(BF16)
    (hx,) = _inproj((h,), l1_norm1, modsx, modsc, w_in1, (w_in1.shape[1],), (BF16,), t_len, s_len)
    tabs_gqa = _rope_tables(_grid_angles(t_len, GQA_DH), 0, ctx_len)
    q_gqa, k_gqa, v_gqa = _gqa_prep(hx, tabs_gqa, l1_q_norm[hc], l1_k_norm[hc])
    att = _attention(q_gqa, k_gqa, v_gqa, n_groups=GQA_KV_HEADS, n_heads=GQA_GROUP, k_shared=True,
                     v_pair=False, t_len=t_len, latent_queries=True)
    return _finish((h,), (att,), modsx, modsc, l1_w_out.astype(BF16), l1_norm2,
                   l1_w1.astype(BF16), l1_w2.astype(BF16), t_len=t_len, n_rows=t_len,
                   final_norm=final_norm)
```

```python
import functools

import numpy as np
import jax
import jax.numpy as jnp
from jax import lax
from jax.experimental import pallas as pl
from jax.experimental.pallas import tpu as pltpu

F32 = jnp.float32
BF16 = jnp.bfloat16

GRID_W = 64
NORM_EPS = 1e-6
ROPE_THETA = 10000.0
N_MOD = 6
MLA_HEADS = 8
MLA_Q_RANK = 256
MLA_KV_RANK = 128
MLA_NOPE = 64
MLA_ROPE = 32
MLA_V = 64
ML_HEADS = 4
ML_DK = 128
ML_DV = 128
ML_CHUNK = 128
N_GATES = 4 * ML_HEADS
GQA_HEADS = 8
GQA_KV_HEADS = 2
GQA_GROUP = GQA_HEADS // GQA_KV_HEADS
GQA_DH = 128

LANES = 128
V7X_VMEM_BYTES = 64 * 1024 * 1024
VMEM_LIMIT = V7X_VMEM_BYTES * 7 // 8

ROW_TILE = 256
ATT_TQ = 1024
ATT_TK = 256
ATT_UNROLL = 8
KPE_OFF = 0
GATE_OFF = 32


def _params(sem):
    return pltpu.CompilerParams(dimension_semantics=sem, vmem_limit_bytes=VMEM_LIMIT)


def _rms(x, g):
    var = jnp.mean(x * x, axis=-1, keepdims=True)
    return x * lax.rsqrt(var + NORM_EPS) * g


def _const_spec(shape):
    nd = len(shape)
    return pl.BlockSpec(shape, lambda *_: (0,) * nd)


def _adaln_kernel(c_ref, w_ref, b_ref, o_ref):
    c = c_ref[...]
    a = c * jax.nn.sigmoid(c)
    o_ref[...] = jnp.dot(a.astype(BF16), w_ref[...].astype(BF16),
                         preferred_element_type=F32) + b_ref[...]


def _adaln(cv, w, b):
    d = cv.shape[1]
    n = w.shape[1]
    tn = n // 4
    return pl.pallas_call(
        _adaln_kernel,
        grid=(n // tn,),
        in_specs=[pl.BlockSpec((8, d), lambda j: (0, 0)),
                  pl.BlockSpec((d, tn), lambda j: (0, j)),
                  pl.BlockSpec((1, tn), lambda j: (0, j))],
        out_specs=pl.BlockSpec((8, tn), lambda j: (0, j)),
        out_shape=jax.ShapeDtypeStruct((8, n), F32),
        compiler_params=_params(("arbitrary",)),
        name="adaln",
    )(cv, w, b.reshape(1, n))


def _row_is_ctx(tile, tm, t_len):
    row = tile * tm + lax.broadcasted_iota(jnp.int32, (tm, 1), 0)
    return row >= t_len


def _stream_specs(hs, tm, t_len):
    d = hs[0].shape[-1]
    if len(hs) == 1:
        return [pl.BlockSpec((1, tm, d), lambda b, i: (b, i, 0))]
    n_lat = t_len // tm
    return [pl.BlockSpec((1, tm, d), lambda b, i: (b, jnp.minimum(i, n_lat - 1), 0)),
            pl.BlockSpec((1, tm, d), lambda b, i: (b, jnp.maximum(i - n_lat, 0), 0))]


def _stream_tile(h_refs, is_ctx):
    if len(h_refs) == 1:
        return h_refs[0][0]
    return jnp.where(is_ctx, h_refs[1][0], h_refs[0][0])


def _inproj_kernel(*refs, n_stream, tm, t_len, widths):
    h_refs, (g_ref, mx_ref, mc_ref, w_ref), out_refs = (
        refs[:n_stream], refs[n_stream:n_stream + 4], refs[n_stream + 4:])
    is_ctx = _row_is_ctx(pl.program_id(1), tm, t_len)
    shift = jnp.where(is_ctx, mc_ref[0:1, :], mx_ref[0, 0:1, :])
    scale = jnp.where(is_ctx, mc_ref[1:2, :], mx_ref[0, 1:2, :])
    y = (_rms(_stream_tile(h_refs, is_ctx), g_ref[...]) * (1.0 + scale) + shift).astype(BF16)
    off = 0
    for o_ref, width in zip(out_refs, widths):
        o_ref[0] = jnp.dot(y, w_ref[:, off:off + width],
                           preferred_element_type=F32).astype(o_ref.dtype)
        off += width


def _inproj(hs, g, modsx, modsc, w, widths, dtypes, t_len, s_len):
    bsz, _, d = hs[0].shape
    tm = ROW_TILE
    kern = functools.partial(_inproj_kernel, n_stream=len(hs), tm=tm, t_len=t_len, widths=widths)
    return pl.pallas_call(
        kern,
        grid=(bsz, s_len // tm),
        in_specs=_stream_specs(hs, tm, t_len) + [
            _const_spec((1, d)),
            pl.BlockSpec((1, 8, d), lambda b, i: (b, 0, 0)),
            _const_spec((8, d)),
            _const_spec(w.shape)],
        out_specs=[pl.BlockSpec((1, tm, wd), lambda b, i: (b, i, 0)) for wd in widths],
        out_shape=[jax.ShapeDtypeStruct((bsz, s_len, wd), dt) for wd, dt in zip(widths, dtypes)],
        compiler_params=_params(("parallel", "parallel")),
        name="inproj",
    )(*hs, g.reshape(1, d), modsx, modsc, w)


def _rope(x, c, s1, s2, half):
    return x * c + pltpu.roll(x, half, 1) * s1 + pltpu.roll(x, LANES - half, 1) * s2


LOG2E = 1.4426950408889634
BF16_SUBLANES = 16
VT_ROWS = LANES + BF16_SUBLANES


def _store_vt(vt_ref, vt, tm):
    for g in range(vt.shape[0] // LANES):
        vt_ref[0, g * VT_ROWS:g * VT_ROWS + LANES, :] = vt[g * LANES:(g + 1) * LANES].astype(BF16)
        vt_ref[0, g * VT_ROWS + LANES:(g + 1) * VT_ROWS, :] = jnp.ones((BF16_SUBLANES, tm), BF16)


def _mla_prep_kernel(lat_ref, side_ref, c_ref, s1_ref, s2_ref, qn_ref, kvn_ref,
                     wq_ref, wk_ref, wv_ref, qt_ref, k_ref, vt_ref, *, scale):
    lat = lat_ref[0].astype(F32)
    cqn = _rms(lat[:, :MLA_Q_RANK], qn_ref[...]).astype(BF16)
    ckvn = _rms(lat[:, MLA_Q_RANK:], kvn_ref[...]).astype(BF16)
    c, s1, s2 = c_ref[...], s1_ref[...], s2_ref[...]
    q = jnp.dot(cqn, wq_ref[...], preferred_element_type=F32)
    kin = jnp.concatenate([ckvn, side_ref[0].astype(BF16)], axis=-1)
    k = jnp.dot(kin, wk_ref[...], preferred_element_type=F32)
    half = MLA_ROPE // 2
    for h in range(MLA_HEADS):
        sl = slice(h * LANES, (h + 1) * LANES)
        qt_ref[0, sl, :] = (_rope(q[:, sl], c, s1, s2, half) * scale).T.astype(BF16)
        k_ref[0, :, sl] = _rope(k[:, sl], c, s1, s2, half).astype(BF16)
    _store_vt(vt_ref, jnp.dot(ckvn, wv_ref[...], preferred_element_type=F32).T, lat.shape[0])


def _mla_prep(lat, side, tabs, qn, kvn, wq, wk, wv):
    bsz, s_len, lw = lat.shape
    tm = ROW_TILE
    hw = MLA_HEADS * LANES
    vw = MLA_HEADS * MLA_V
    tok = lambda w: pl.BlockSpec((1, tm, w), lambda b, i: (b, i, 0))
    tok_t = lambda w: pl.BlockSpec((1, w, tm), lambda b, i: (b, 0, i))
    tab = pl.BlockSpec((tm, LANES), lambda b, i: (i, 0))
    kern = functools.partial(_mla_prep_kernel, scale=LOG2E * (MLA_NOPE + MLA_ROPE) ** -0.5)
    return pl.pallas_call(
        kern,
        grid=(bsz, s_len // tm),
        in_specs=[tok(lw), tok(LANES), tab, tab, tab,
                  _const_spec((1, MLA_Q_RANK)), _const_spec((1, MLA_KV_RANK)),
                  _const_spec(wq.shape), _const_spec(wk.shape), _const_spec(wv.shape)],
        out_specs=[tok_t(hw), tok(hw), tok_t(vw // LANES * VT_ROWS)],
        out_shape=[jax.ShapeDtypeStruct((bsz, hw, s_len), BF16),
                   jax.ShapeDtypeStruct((bsz, s_len, hw), BF16),
                   jax.ShapeDtypeStruct((bsz, vw // LANES * VT_ROWS, s_len), BF16)],
        compiler_params=_params(("parallel", "parallel")),
        name="mla_prep",
    )(lat, side, *tabs, qn.reshape(1, -1), kvn.reshape(1, -1), wq, wk, wv)


def _gqa_prep_kernel(q_in, k_in, v_in, c_ref, s1_ref, s2_ref, qn_ref, kn_ref,
                     qt_ref, k_ref, vt_ref, *, scale):
    c, s1, s2 = c_ref[...], s1_ref[...], s2_ref[...]
    half = GQA_DH // 2
    for h in range(GQA_HEADS):
        sl = slice(h * LANES, (h + 1) * LANES)
        x = _rms(q_in[0, :, sl].astype(F32), qn_ref[...])
        qt_ref[0, sl, :] = (_rope(x, c, s1, s2, half) * scale).T.astype(BF16)
    for h in range(GQA_KV_HEADS):
        sl = slice(h * LANES, (h + 1) * LANES)
        x = _rms(k_in[0, :, sl].astype(F32), kn_ref[...])
        k_ref[0, :, sl] = _rope(x, c, s1, s2, half).astype(BF16)
    _store_vt(vt_ref, v_in[0].astype(F32).T, v_in.shape[1])


def _gqa_prep(hx, tabs, qn, kn):
    bsz, s_len, _ = hx.shape
    tm = ROW_TILE
    qw = GQA_HEADS * GQA_DH
    kw = GQA_KV_HEADS * GQA_DH
    tab = pl.BlockSpec((tm, LANES), lambda b, i: (i, 0))
    kern = functools.partial(_gqa_prep_kernel, scale=LOG2E * GQA_DH ** -0.5)
    return pl.pallas_call(
        kern,
        grid=(bsz, s_len // tm),
        in_specs=[pl.BlockSpec((1, tm, qw), lambda b, i: (b, i, 0)),
                  pl.BlockSpec((1, tm, kw), lambda b, i: (b, i, qw // kw)),
                  pl.BlockSpec((1, tm, kw), lambda b, i: (b, i, qw // kw + 1)),
                  tab, tab, tab, _const_spec((1, GQA_DH)), _const_spec((1, GQA_DH))],
        out_specs=[pl.BlockSpec((1, qw, tm), lambda b, i: (b, 0, i)),
                   pl.BlockSpec((1, tm, kw), lambda b, i: (b, i, 0)),
                   pl.BlockSpec((1, kw // LANES * VT_ROWS, tm), lambda b, i: (b, 0, i))],
        out_shape=[jax.ShapeDtypeStruct((bsz, qw, s_len), BF16),
                   jax.ShapeDtypeStruct((bsz, s_len, kw), BF16),
                   jax.ShapeDtypeStruct((bsz, kw // LANES * VT_ROWS, s_len), BF16)],
        compiler_params=_params(("parallel", "parallel")),
        name="gqa_prep",
    )(hx, hx, hx, *tabs, qn.reshape(1, -1), kn.reshape(1, -1))


def _attn_kernel(qt_ref, k_ref, vt_ref, o_ref, m_sc, l_sc, acc_sc, *,
                 n_heads, k_shared, v_pair, latent_queries, tq, tk, t_len, s_len):
    def scores(g, kc):
        kg = kc if k_shared else kc[:, g * LANES:(g + 1) * LANES]
        return jnp.dot(kg, qt_ref[0, g * LANES:(g + 1) * LANES, :], preferred_element_type=F32)

    def k_chunk(j):
        return k_ref[0, pl.ds(pl.multiple_of(j * tk, tk), tk), :]

    def update(g, st, vte, first):
        m_new = jnp.max(st, axis=0, keepdims=True)
        if not first:
            m_old = m_sc[g][0:1]
            m_new = jnp.maximum(m_old, m_new)
            alpha = jnp.exp2(m_old - m_new)
        p = jnp.exp2(st - m_new).astype(BF16)
        res = jnp.dot(vte, p, preferred_element_type=F32)
        if first:
            acc_sc[g] = res[:LANES]
            l_sc[g] = res[LANES:LANES + 8]
        else:
            acc_sc[g] = alpha * acc_sc[g] + res[:LANES]
            l_sc[g] = alpha * l_sc[g] + res[LANES:LANES + 8]
        m_sc[g] = jnp.broadcast_to(m_new, (8, tq))

    kc = k_ref[0, t_len:s_len, :]
    vte = vt_ref[0, :, t_len:s_len]
    st_next = scores(0, kc)
    for g in range(n_heads):
        st = st_next
        if g + 1 < n_heads:
            st_next = scores(g + 1, kc)
        elif latent_queries:
            st_next = scores(0, k_chunk(0))
        update(g, st, vte, True)

    if latent_queries:
        n_chunks = t_len // tk

        def body(j, st_next):
            kc = k_chunk(j)
            vte = vt_ref[0, :, pl.ds(pl.multiple_of(j * tk, tk), tk)]
            for g in range(n_heads):
                st = st_next
                if g + 1 < n_heads:
                    st_next = scores(g + 1, kc)
                else:
                    st_next = scores(0, k_chunk(jnp.minimum(j + 1, n_chunks - 1)))
                update(g, st, vte, False)
            return st_next

        lax.fori_loop(0, n_chunks, body, st_next, unroll=ATT_UNROLL)

    outs = [acc_sc[g] / l_sc[g][0:1] for g in range(n_heads)]
    if v_pair:
        row = lax.broadcasted_iota(jnp.int32, (LANES, tq), 0)
        o_ref[0] = jnp.where(row < LANES // 2, outs[0], outs[1]).T.astype(o_ref.dtype)
    else:
        for g in range(n_heads):
            o_ref[0, :, g * LANES:(g + 1) * LANES] = outs[g].T.astype(o_ref.dtype)


def _attention(qt, k, vt, *, n_groups, n_heads, k_shared, v_pair, t_len, latent_queries):
    bsz, s_len, _ = k.shape
    tq = ATT_TQ if latent_queries else s_len - t_len
    n_q_rows = t_len if latent_queries else s_len - t_len
    q0 = 0 if latent_queries else t_len // tq
    tk = ATT_TK
    qw = n_heads * LANES
    kw = LANES if k_shared else qw
    ow = LANES if v_pair else qw
    kern = functools.partial(_attn_kernel, n_heads=n_heads, k_shared=k_shared, v_pair=v_pair,
                             latent_queries=latent_queries, tq=tq, tk=tk, t_len=t_len, s_len=s_len)
    return pl.pallas_call(
        kern,
        grid=(bsz, n_groups, n_q_rows // tq),
        in_specs=[pl.BlockSpec((1, qw, tq), lambda b, g, i: (b, g, q0 + i)),
                  pl.BlockSpec((1, s_len, kw), lambda b, g, i: (b, 0, g)),
                  pl.BlockSpec((1, VT_ROWS, s_len), lambda b, g, i: (b, g, 0))],
        out_specs=pl.BlockSpec((1, tq, ow), lambda b, g, i: (b, i, g)),
        out_shape=jax.ShapeDtypeStruct((bsz, n_q_rows, n_groups * ow), BF16),
        scratch_shapes=[pltpu.VMEM((n_heads, 8, tq), F32),
                        pltpu.VMEM((n_heads, 8, tq), F32),
                        pltpu.VMEM((n_heads, LANES, tq), F32)],
        compiler_params=_params(("parallel", "parallel", "arbitrary")),
        name="attention",
    )(qt, k, vt)


def _conv_kernel(cur_ref, prev_ref, next_ref, w_ref, o_ref, *, tm, t_len, s_len, qscale):
    i = pl.program_id(1)
    x = cur_ref[0].astype(F32)
    row = i * tm + lax.broadcasted_iota(jnp.int32, (tm, 1), 0)
    local = lax.broadcasted_iota(jnp.int32, (tm, 1), 0)
    prev_row = prev_ref[0, 15:16, :].astype(F32)
    next_row = next_ref[0, 0:1, :].astype(F32)
    xm1 = jnp.where(local == 0, prev_row, pltpu.roll(x, 1, 0))
    xp1 = jnp.where(local == tm - 1, next_row, pltpu.roll(x, tm - 1, 0))
    xm1 = jnp.where((row == 0) | (row == t_len), 0.0, xm1)
    xp1 = jnp.where((row == t_len - 1) | (row == s_len - 1), 0.0, xp1)
    y = w_ref[0:1, :] * xm1 + w_ref[1:2, :] * x + w_ref[2:3, :] * xp1
    y = y * jax.nn.sigmoid(y)
    half = y.shape[1] // 2
    o_ref[0, :, :half] = (y[:, :half] * qscale).astype(BF16)
    o_ref[0, :, half:] = y[:, half:].astype(BF16)


def _conv_silu(qk_pre, conv_w, t_len):
    bsz, s_len, w = qk_pre.shape
    tm = ROW_TILE
    r16 = tm // 16
    n16 = s_len // 16
    kern = functools.partial(_conv_kernel, tm=tm, t_len=t_len, s_len=s_len, qscale=ML_DK ** -0.5)
    return pl.pallas_call(
        kern,
        grid=(bsz, s_len // tm),
        in_specs=[pl.BlockSpec((1, tm, w), lambda b, i: (b, i, 0)),
                  pl.BlockSpec((1, 16, w), lambda b, i: (b, jnp.maximum(i * r16 - 1, 0), 0)),
                  pl.BlockSpec((1, 16, w), lambda b, i: (b, jnp.minimum((i + 1) * r16, n16 - 1), 0)),
                  _const_spec((8, w))],
        out_specs=pl.BlockSpec((1, tm, w), lambda b, i: (b, i, 0)),
        out_shape=jax.ShapeDtypeStruct((bsz, s_len, w), BF16),
        compiler_params=_params(("parallel", "parallel")),
        name="conv_silu",
    )(qk_pre, qk_pre, qk_pre, jnp.pad(conv_w, ((0, 8 - conv_w.shape[0]), (0, 0))))


def _split3(a):
    a1 = a.astype(BF16)
    r1 = a - a1.astype(F32)
    a2 = r1.astype(BF16)
    a3 = (r1 - a2.astype(F32)).astype(BF16)
    return a1, a2, a3


def _mlstm_kernel(qkf_ref, qkb_ref, vf_ref, vb_ref, gf_ref, gb_ref, bias_ref,
                  hf_ref, hb_ref, c_sc, m_sc):
    L = ML_CHUNK

    @pl.when(pl.program_id(1) == 0)
    def _():
        c_sc[...] = jnp.zeros_like(c_sc)
        m_sc[...] = jnp.zeros_like(m_sc)

    r = lax.broadcasted_iota(jnp.int32, (L, L), 0)
    c = lax.broadcasted_iota(jnp.int32, (L, L), 1)
    lane = lax.broadcasted_iota(jnp.int32, (1, LANES), 1) - GATE_OFF
    is_forget = (lane >= 0) & (lane < N_GATES) & ((lane % (2 * ML_HEADS)) >= ML_HEADS)
    ones = jnp.ones((L, ML_DV), BF16)
    dirs = ((qkf_ref, vf_ref, gf_ref, hf_ref), (qkb_ref, vb_ref, gb_ref, hb_ref))
    for d, (qk_ref, v_ref, g_ref, h_ref) in enumerate(dirs):
        seen = (c <= r) if d == 0 else (c >= r)
        seen_b = seen.astype(F32).astype(BF16)
        last = L - 1 if d == 0 else 0
        gates = g_ref[0] + bias_ref[...]
        soft = jnp.minimum(gates, 0.0) - jnp.log1p(jnp.exp(-jnp.abs(gates)))
        logs = jnp.where(is_forget, soft, gates)
        logs_t = logs.T[GATE_OFF:GATE_OFF + N_GATES, :]
        cum_col = sum(jnp.dot(seen_b, p, preferred_element_type=F32) for p in _split3(logs))
        cum_row = sum(lax.dot_general(p, seen_b, (((1,), (1,)), ((), ())), preferred_element_type=F32)
                      for p in _split3(logs_t))
        for h in range(ML_HEADS):
            gi = d * 2 * ML_HEADS + h
            gf = gi + ML_HEADS
            b_col = cum_col[:, GATE_OFF + gf:GATE_OFF + gf + 1]
            b_row = cum_row[gf:gf + 1, :]
            li_col = logs[:, GATE_OFF + gi:GATE_OFF + gi + 1]
            li_row = logs_t[gi:gi + 1, :]
            b_end = b_row[:, last:last + 1]
            m_prev = m_sc[d, h][0:1, 0:1]
            sl = slice(h * LANES, (h + 1) * LANES)
            q = qk_ref[0, :, sl]
            k = qk_ref[0, :, ML_HEADS * ML_DK + h * LANES:ML_HEADS * ML_DK + (h + 1) * LANES]
            vext = jnp.concatenate([v_ref[0, :, sl], ones], axis=-1)
            state = c_sc[d, h]

            logw = jnp.where(seen, b_col - b_row + li_row, -jnp.inf)
            m_t = jnp.maximum(b_col + m_prev, jnp.max(logw, axis=-1, keepdims=True))
            w_state = jnp.exp(b_col + m_prev - m_t)
            qk = lax.dot_general(q, k, (((1,), (1,)), ((), ())), preferred_element_type=F32)
            s = qk * jnp.exp(logw - m_t)
            inter = jnp.dot(q, state.astype(BF16), preferred_element_type=F32)
            intra = jnp.dot(s.astype(BF16), vext, preferred_element_type=F32)
            mixed = w_state * inter + intra
            den = jnp.maximum(jnp.abs(mixed[:, ML_DV:]), jnp.exp(-m_t))
            h_ref[0, :, sl] = mixed[:, :ML_DV] / den

            g_col = b_end - b_col + li_col
            m_new = jnp.maximum(b_end + m_prev, jnp.max(g_col, axis=0, keepdims=True))
            decay = jnp.exp(b_end + m_prev - m_new)
            wk = (jnp.exp(g_col - m_new) * k.astype(F32)).astype(BF16)
            upd = lax.dot_general(wk, vext, (((0,), (0,)), ((), ())), preferred_element_type=F32)
            c_sc[d, h] = decay * state + upd
            m_sc[d, h] = jnp.broadcast_to(m_new, (8, LANES))


def _mlstm(qk, v, side, gate_bias_row, t_len):
    bsz, s_len, _ = qk.shape
    L = ML_CHUNK
    n_chunks = s_len // L
    n_lat = t_len // L
    hw = ML_HEADS * ML_DV
    fwd = lambda b, j: (b, (j + n_lat) % n_chunks, 0)
    bwd = lambda b, j: (b, n_chunks - 1 - j, 0)
    return pl.pallas_call(
        _mlstm_kernel,
        grid=(bsz, n_chunks),
        in_specs=[pl.BlockSpec((1, L, 2 * hw), fwd), pl.BlockSpec((1, L, 2 * hw), bwd),
                  pl.BlockSpec((1, L, hw), fwd), pl.BlockSpec((1, L, hw), bwd),
                  pl.BlockSpec((1, L, LANES), fwd), pl.BlockSpec((1, L, LANES), bwd),
                  _const_spec((1, LANES))],
        out_specs=[pl.BlockSpec((1, L, hw), fwd), pl.BlockSpec((1, L, hw), bwd)],
        out_shape=[jax.ShapeDtypeStruct((bsz, s_len, hw), F32)] * 2,
        scratch_shapes=[pltpu.VMEM((2, ML_HEADS, ML_DK, 2 * ML_DV), F32),
                        pltpu.VMEM((2, ML_HEADS, 8, LANES), F32)],
        compiler_params=_params(("parallel", "arbitrary")),
        name="mlstm",
    )(qk, qk, v, v, side, side, gate_bias_row)


def _finish_kernel(*refs, n_stream, tm, t_len, with_readout, with_final, ff_chunk):
    h_refs = refs[:n_stream]
    it = iter(refs[n_stream:])
    if with_readout:
        alat_ref, actx_ref, hf_ref, hb_ref, o_ref, hn_ref = (next(it) for _ in range(6))
    else:
        mix_ref = next(it)
    mx_ref, mc_ref, wo_ref, n2_ref, w1_ref, w2_ref = (next(it) for _ in range(6))
    fn_ref = next(it) if with_final else None
    out_ref = next(it)

    is_ctx = _row_is_ctx(pl.program_id(1), tm, t_len)
    mod = lambda j: jnp.where(is_ctx, mc_ref[j:j + 1, :], mx_ref[0, j:j + 1, :])

    if with_readout:
        hsum = hf_ref[0] + hb_ref[0]
        parts = [_rms(hsum[:, h * ML_DV:(h + 1) * ML_DV], hn_ref[:, h * ML_DV:(h + 1) * ML_DV])
                 for h in range(ML_HEADS)]
        rec = jax.nn.sigmoid(o_ref[0].astype(F32)) * jnp.concatenate(parts, axis=-1)
        att = jnp.where(is_ctx, actx_ref[0], alat_ref[0])
        mix = jnp.concatenate([att, rec.astype(BF16)], axis=-1)
    else:
        mix = mix_ref[0]
    s = _stream_tile(h_refs, is_ctx) + mod(2) * jnp.dot(mix, wo_ref[...], preferred_element_type=F32)
    y = (_rms(s, n2_ref[...]) * (1.0 + mod(4)) + mod(3)).astype(BF16)
    d_ff = w1_ref.shape[1]
    acc = jnp.zeros(s.shape, F32)
    for c0 in range(0, d_ff, ff_chunk):
        u = jnp.maximum(jnp.dot(y, w1_ref[:, c0:c0 + ff_chunk], preferred_element_type=F32), 0.0)
        acc = acc + jnp.dot((u * u).astype(BF16), w2_ref[c0:c0 + ff_chunk, :],
                            preferred_element_type=F32)
    out = s + mod(5) * acc
    if with_final:
        out = _rms(out, fn_ref[...])
    out_ref[0] = out


def _finish(hs, mix_parts, modsx, modsc, w_out, norm2, w1, w2, *, t_len, n_rows, head_norm=None,
            final_norm=None):
    bsz, _, d = hs[0].shape
    tm = ROW_TILE
    ns = len(hs)
    with_readout = head_norm is not None
    with_final = final_norm is not None
    tok = lambda w: pl.BlockSpec((1, tm, w), lambda b, i: (b, i, 0))
    once = lambda shape: pl.BlockSpec(shape, lambda *_: (0,) * len(shape),
                                      pipeline_mode=pl.Buffered(1))
    args = list(hs) + list(mix_parts)
    specs = _stream_specs(hs, tm, t_len) + [tok(p.shape[-1]) for p in mix_parts]
    if with_readout:
        n_lat = t_len // tm
        aw = mix_parts[0].shape[-1]
        specs[ns] =pl.BlockSpec((1, tm, aw), lambda b, i: (b, jnp.minimum(i, n_lat - 1), 0))
        specs[ns + 1] =pl.BlockSpec((1, tm, aw), lambda b, i: (b, jnp.maximum(i - n_lat, 0), 0))
        args.append(head_norm.reshape(1, -1))
        specs.append(_const_spec((1, head_norm.shape[0])))
    args += [modsx, modsc, w_out, norm2.reshape(1, d), w1, w2]
    specs += [pl.BlockSpec((1, 8, d), lambda b, i: (b, 0, 0)), _const_spec((8, d)),
              once(w_out.shape), _const_spec((1, d)), once(w1.shape), once(w2.shape)]
    if with_final:
        args.append(final_norm.reshape(1, d))
        specs.append(_const_spec((1, d)))
    kern = functools.partial(_finish_kernel, n_stream=ns, tm=tm, t_len=t_len, with_readout=with_readout,
                             with_final=with_final, ff_chunk=1024)
    return pl.pallas_call(
        kern,
        grid=(bsz, n_rows // tm),
        in_specs=specs,
        out_specs=tok(d),
        out_shape=jax.ShapeDtypeStruct((bsz, n_rows, d), F32),
        compiler_params=_params(("parallel", "parallel")),
        name="finish",
    )(*args)


def _rope_tables(t_len, ctx_len, d_rot, lane0):
    n_freq = d_rot // 4
    n = 2 * n_freq
    rel = np.arange(LANES) - lane0
    first = (rel >= 0) & (rel < n)
    second = (rel >= n) & (rel < 2 * n)
    idx = np.clip(np.where(second, rel - n, rel), 0, n - 1)
    freqs = ROPE_THETA ** (-jnp.arange(n_freq, dtype=F32) / n_freq)
    t = jnp.arange(t_len + ctx_len)[:, None]
    pos = jnp.where(jnp.asarray(idx // n_freq == 0)[None, :], t // GRID_W, t % GRID_W).astype(F32)
    ang = pos * freqs[idx % n_freq][None, :]
    live = t < t_len
    cos, sin = jnp.cos(ang), jnp.sin(ang)
    c = jnp.where(live & jnp.asarray(first | second)[None, :], cos, 1.0)
    s1 = jnp.where(live & jnp.asarray(second)[None, :], sin, 0.0)
    s2 = jnp.where(live & jnp.asarray(first)[None, :], -sin, 0.0)
    return c, s1, s2


def _mla_rope_cols():
    nf = MLA_ROPE // 4
    first = [a * 2 * nf + f for a in range(2) for f in range(nf)]
    second = [a * 2 * nf + nf + f for a in range(2) for f in range(nf)]
    return np.array(first + second)


def _gqa_head_cols():
    nf = GQA_DH // 4
    first = [a * 2 * nf + f for a in range(2) for f in range(nf)]
    second = [a * 2 * nf + nf + f for a in range(2) for f in range(nf)]
    return np.array(first + second)


def kernel(x, c, ctx, c_ctx,
           l0_ada_w, l0_ada_b, l0_norm1, l0_w_in, l0_mla_q_norm, l0_mla_w_uq, l0_mla_kv_norm, l0_mla_w_ukv,
           l0_ml_conv, l0_ml_gate_b, l0_ml_head_norm, l0_w_out, l0_norm2, l0_w1, l0_w2,
           l1_ada_w, l1_ada_b, l1_norm1, l1_w_in, l1_q_norm, l1_k_norm, l1_w_out, l1_norm2, l1_w1, l1_w2,
           final_norm):
    bsz, t_len, d = x.shape
    ctx_len = ctx.shape[1]
    s_len = t_len + ctx_len

    cv = jnp.zeros((8, d), F32).at[:bsz].set(c).at[bsz].set(c_ctx)

    def mods(ada_w, ada_b):
        m = _adaln(cv, ada_w, ada_b).reshape(8, N_MOD, d)
        pad = jnp.zeros((8 - N_MOD, d), F32)
        modsx = jnp.concatenate([m[:bsz], jnp.broadcast_to(pad, (bsz,) + pad.shape)], axis=1)
        modsc = jnp.concatenate([m[bsz], pad], axis=0)
        return modsx, modsc

    modsx, modsc = mods(l0_ada_w, l0_ada_b)
    o = np.cumsum([0, MLA_Q_RANK, MLA_KV_RANK, MLA_ROPE, 2 * ML_HEADS * ML_DK, ML_HEADS * ML_DV,
                   ML_HEADS * ML_DV, N_GATES])
    w_cq, w_ckv, w_kpe, w_qk, w_v, w_o, w_g = (l0_w_in[:, o[i]:o[i + 1]] for i in range(7))
    side_pad = jnp.zeros((d, LANES - MLA_ROPE - N_GATES), F32)
    w_in0 = jnp.concatenate([w_qk, w_v, w_o, w_cq, w_ckv, w_kpe[:, _mla_rope_cols()], w_g, side_pad],
                            axis=1).astype(BF16)
    widths0 = (2 * ML_HEADS * ML_DK, ML_HEADS * ML_DV, ML_HEADS * ML_DV, MLA_Q_RANK + MLA_KV_RANK, LANES)
    qk_pre, v_ml, o_pre, lat, side = _inproj((x, ctx), l0_norm1, modsx, modsc, w_in0, widths0,
                                             (BF16, BF16, BF16, BF16, F32), t_len, s_len)

    dq = MLA_NOPE + MLA_ROPE
    head_cols = np.concatenate([np.arange(MLA_NOPE), MLA_NOPE + _mla_rope_cols()])
    wq = l0_mla_w_uq.reshape(MLA_Q_RANK, MLA_HEADS, dq)[:, :, head_cols]
    wq = jnp.pad(wq, ((0, 0), (0, 0), (0, LANES - dq))).reshape(MLA_Q_RANK, MLA_HEADS * LANES).astype(BF16)
    wkv = l0_mla_w_ukv.reshape(MLA_KV_RANK, MLA_HEADS, MLA_NOPE + MLA_V)
    wk_nope = jnp.pad(wkv[:, :, :MLA_NOPE], ((0, 0), (0, 0), (0, LANES - MLA_NOPE)))
    place = np.zeros((LANES, MLA_HEADS, LANES), np.float32)
    for hh in range(MLA_HEADS):
        place[KPE_OFF + np.arange(MLA_ROPE), hh, MLA_NOPE + np.arange(MLA_ROPE)] = 1.0
    wk = jnp.concatenate([wk_nope, jnp.asarray(place)], axis=0).reshape(
        MLA_KV_RANK + LANES, MLA_HEADS * LANES).astype(BF16)
    wv = wkv[:, :, MLA_NOPE:].reshape(MLA_KV_RANK, MLA_HEADS * MLA_V).astype(BF16)
    tabs_mla = _rope_tables(t_len, ctx_len, MLA_ROPE, MLA_NOPE)
    q_mla, k_mla, v_mla = _mla_prep(lat, side, tabs_mla, l0_mla_q_norm, l0_mla_kv_norm, wq, wk, wv)
    mla_att = functools.partial(_attention, q_mla, k_mla, v_mla, n_groups=MLA_HEADS // 2, n_heads=2,
                                k_shared=False, v_pair=True, t_len=t_len)
    a_lat, a_ctx = mla_att(latent_queries=True), mla_att(latent_queries=False)

    qk = _conv_silu(qk_pre, l0_ml_conv, t_len)
    bias_row = jnp.zeros((1, LANES), F32).at[0, GATE_OFF:GATE_OFF + N_GATES].set(l0_ml_gate_b)
    hf, hb = _mlstm(qk, v_ml, side, bias_row, t_len)
    h = _finish((x, ctx), (a_lat, a_ctx, hf, hb, o_pre), modsx, modsc, l0_w_out.astype(BF16),
                l0_norm2, l0_w1.astype(BF16), l0_w2.astype(BF16), t_len=t_len, n_rows=s_len,
                head_norm=l0_ml_head_norm)

    modsx, modsc = mods(l1_ada_w, l1_ada_b)
    hc = _gqa_head_cols()
    n_qk = GQA_HEADS + GQA_KV_HEADS
    qk_cols = (np.arange(n_qk)[:, None] * GQA_DH + hc[None, :]).reshape(-1)
    w_in1 = jnp.concatenate([l1_w_in[:, qk_cols], l1_w_in[:, n_qk * GQA_DH:]], axis=1).astype(BF16)
    (hx,) = _inproj((h,), l1_norm1, modsx, modsc, w_in1, (w_in1.shape[1],), (BF16,), t_len, s_len)
    tabs_gqa = _rope_tables(t_len, ctx_len, GQA_DH, 0)
    q_gqa, k_gqa, v_gqa = _gqa_prep(hx, tabs_gqa, l1_q_norm[hc], l1_k_norm[hc])
    att = _attention(q_gqa, k_gqa, v_gqa, n_groups=GQA_KV_HEADS, n_heads=GQA_GROUP, k_shared=True,
                     v_pair=False, t_len=t_len, latent_queries=True)
    return _finish((h,), (att,), modsx, modsc, l1_w_out.astype(BF16), l1_norm2,
                   l1_w1.astype(BF16), l1_w2.astype(BF16), t_len=t_len, n_rows=t_len,
                   final_norm=final_norm)
```

```python
import functools

import numpy as np
import jax
import jax.numpy as jnp
from jax import lax
from jax.experimental import pallas as pl
from jax.experimental.pallas import tpu as pltpu

F32 = jnp.float32
BF16 = jnp.bfloat16

GRID_W = 64
NORM_EPS = 1e-6
ROPE_THETA = 10000.0
N_MOD = 6
MLA_HEADS = 8
MLA_Q_RANK = 256
MLA_KV_RANK = 128
MLA_NOPE = 64
MLA_ROPE = 32
MLA_V = 64
ML_HEADS = 4
ML_DK = 128
ML_DV = 128
ML_CHUNK = 128
N_GATES = 4 * ML_HEADS
GQA_HEADS = 8
GQA_KV_HEADS = 2
GQA_GROUP = GQA_HEADS // GQA_KV_HEADS
GQA_DH = 128

LANES = 128
V7X_VMEM_BYTES = 64 * 1024 * 1024
VMEM_LIMIT = V7X_VMEM_BYTES * 7 // 8

ROW_TILE = 256
ATT_TQ = 1024
ATT_TK = 256
ATT_UNROLL = 8
KPE_OFF = 0
GATE_OFF = 32


def _params(sem):
    return pltpu.CompilerParams(dimension_semantics=sem, vmem_limit_bytes=VMEM_LIMIT)


def _rms(x, g):
    var = jnp.mean(x * x, axis=-1, keepdims=True)
    return x * lax.rsqrt(var + NORM_EPS) * g


def _const_spec(shape):
    nd = len(shape)
    return pl.BlockSpec(shape, lambda *_: (0,) * nd)


def _adaln_kernel(c_ref, w_ref, b_ref, o_ref):
    c = c_ref[...]
    a = c * jax.nn.sigmoid(c)
    o_ref[...] = jnp.dot(a.astype(BF16), w_ref[...].astype(BF16),
                         preferred_element_type=F32) + b_ref[...]


def _adaln(cv, w, b):
    d = cv.shape[1]
    n = w.shape[1]
    tn = n // 4
    return pl.pallas_call(
        _adaln_kernel,
        grid=(n // tn,),
        in_specs=[pl.BlockSpec((8, d), lambda j: (0, 0)),
                  pl.BlockSpec((d, tn), lambda j: (0, j)),
                  pl.BlockSpec((1, tn), lambda j: (0, j))],
        out_specs=pl.BlockSpec((8, tn), lambda j: (0, j)),
        out_shape=jax.ShapeDtypeStruct((8, n), F32),
        compiler_params=_params(("arbitrary",)),
        name="adaln",
    )(cv, w, b.reshape(1, n))


def _row_is_ctx(tile, tm, t_len):
    row = tile * tm + lax.broadcasted_iota(jnp.int32, (tm, 1), 0)
    return row >= t_len


def _stream_specs(hs, tm, t_len):
    d = hs[0].shape[-1]
    if len(hs) == 1:
        return [pl.BlockSpec((1, tm, d), lambda b, i: (b, i, 0))]
    n_lat = t_len // tm
    return [pl.BlockSpec((1, tm, d), lambda b, i: (b, jnp.minimum(i, n_lat - 1), 0)),
            pl.BlockSpec((1, tm, d), lambda b, i: (b, jnp.maximum(i - n_lat, 0), 0))]


def _stream_tile(h_refs, is_ctx):
    if len(h_refs) == 1:
        return h_refs[0][0]
    return jnp.where(is_ctx, h_refs[1][0], h_refs[0][0])


def _inproj_kernel(*refs, n_stream, tm, t_len, widths):
    h_refs, (g_ref, mx_ref, mc_ref, w_ref), out_refs = (
        refs[:n_stream], refs[n_stream:n_stream + 4], refs[n_stream + 4:])
    is_ctx = _row_is_ctx(pl.program_id(1), tm, t_len)
    shift = jnp.where(is_ctx, mc_ref[0:1, :], mx_ref[0, 0:1, :])
    scale = jnp.where(is_ctx, mc_ref[1:2, :], mx_ref[0, 1:2, :])
    y = (_rms(_stream_tile(h_refs, is_ctx), g_ref[...]) * (1.0 + scale) + shift).astype(BF16)
    off = 0
    for o_ref, width in zip(out_refs, widths):
        o_ref[0] = jnp.dot(y, w_ref[:, off:off + width],
                           preferred_element_type=F32).astype(o_ref.dtype)
        off += width


def _inproj(hs, g, modsx, modsc, w, widths, dtypes, t_len, s_len):
    bsz, _, d = hs[0].shape
    tm = ROW_TILE
    kern = functools.partial(_inproj_kernel, n_stream=len(hs), tm=tm, t_len=t_len, widths=widths)
    return pl.pallas_call(
        kern,
        grid=(bsz, s_len // tm),
        in_specs=_stream_specs(hs, tm, t_len) + [
            _const_spec((1, d)),
            pl.BlockSpec((1, 8, d), lambda b, i: (b, 0, 0)),
            _const_spec((8, d)),
            _const_spec(w.shape)],
        out_specs=[pl.BlockSpec((1, tm, wd), lambda b, i: (b, i, 0)) for wd in widths],
        out_shape=[jax.ShapeDtypeStruct((bsz, s_len, wd), dt) for wd, dt in zip(widths, dtypes)],
        compiler_params=_params(("parallel", "parallel")),
        name="inproj",
    )(*hs, g.reshape(1, d), modsx, modsc, w)


def _rope(x, c, s1, s2, half):
    return x * c + pltpu.roll(x, half, 1) * s1 + pltpu.roll(x, LANES - half, 1) * s2


LOG2E = 1.4426950408889634
BF16_SUBLANES = 16
VT_ROWS = LANES + BF16_SUBLANES


def _store_vt(vt_ref, vt, tm):
    for g in range(vt.shape[0] // LANES):
        vt_ref[0, g * VT_ROWS:g * VT_ROWS + LANES, :] = vt[g * LANES:(g + 1) * LANES].astype(BF16)
        vt_ref[0, g * VT_ROWS + LANES:(g + 1) * VT_ROWS, :] = jnp.ones((BF16_SUBLANES, tm), BF16)


def _mla_prep_kernel(lat_ref, side_ref, c_ref, s1_ref, s2_ref, qn_ref, kvn_ref,
                     wq_ref, wk_ref, wv_ref, qt_ref, k_ref, vt_ref, *, scale):
    lat = lat_ref[0].astype(F32)
    cqn = _rms(lat[:, :MLA_Q_RANK], qn_ref[...]).astype(BF16)
    ckvn = _rms(lat[:, MLA_Q_RANK:], kvn_ref[...]).astype(BF16)
    c, s1, s2 = c_ref[...], s1_ref[...], s2_ref[...]
    q = jnp.dot(cqn, wq_ref[...], preferred_element_type=F32)
    kin = jnp.concatenate([ckvn, side_ref[0].astype(BF16)], axis=-1)
    k = jnp.dot(kin, wk_ref[...], preferred_element_type=F32)
    half = MLA_ROPE // 2
    for h in range(MLA_HEADS):
        sl = slice(h * LANES, (h + 1) * LANES)
        qt_ref[0, sl, :] = (_rope(q[:, sl], c, s1, s2, half) * scale).T.astype(BF16)
        k_ref[0, :, sl] = _rope(k[:, sl], c, s1, s2, half).astype(BF16)
    _store_vt(vt_ref, jnp.dot(ckvn, wv_ref[...], preferred_element_type=F32).T, lat.shape[0])


def _mla_prep(lat, side, tabs, qn, kvn, wq, wk, wv):
    bsz, s_len, lw = lat.shape
    tm = ROW_TILE
    hw = MLA_HEADS * LANES
    vw = MLA_HEADS * MLA_V
    tok = lambda w: pl.BlockSpec((1, tm, w), lambda b, i: (b, i, 0))
    tok_t = lambda w: pl.BlockSpec((1, w, tm), lambda b, i: (b, 0, i))
    tab = pl.BlockSpec((tm, LANES), lambda b, i: (i, 0))
    kern = functools.partial(_mla_prep_kernel, scale=LOG2E * (MLA_NOPE + MLA_ROPE) ** -0.5)
    return pl.pallas_call(
        kern,
        grid=(bsz, s_len // tm),
        in_specs=[tok(lw), tok(LANES), tab, tab, tab,
                  _const_spec((1, MLA_Q_RANK)), _const_spec((1, MLA_KV_RANK)),
                  _const_spec(wq.shape), _const_spec(wk.shape), _const_spec(wv.shape)],
        out_specs=[tok_t(hw), tok(hw), tok_t(vw // LANES * VT_ROWS)],
        out_shape=[jax.ShapeDtypeStruct((bsz, hw, s_len), BF16),
                   jax.ShapeDtypeStruct((bsz, s_len, hw), BF16),
                   jax.ShapeDtypeStruct((bsz, vw // LANES * VT_ROWS, s_len), BF16)],
        compiler_params=_params(("parallel", "parallel")),
        name="mla_prep",
    )(lat, side, *tabs, qn.reshape(1, -1), kvn.reshape(1, -1), wq, wk, wv)


def _gqa_prep_kernel(q_in, k_in, v_in, c_ref, s1_ref, s2_ref, qn_ref, kn_ref,
                     qt_ref, k_ref, vt_ref, *, scale):
    c, s1, s2 = c_ref[...], s1_ref[...], s2_ref[...]
    half = GQA_DH // 2
    for h in range(GQA_HEADS):
        sl = slice(h * LANES, (h + 1) * LANES)
        x = _rms(q_in[0, :, sl].astype(F32), qn_ref[...])
        qt_ref[0, sl, :] = (_rope(x, c, s1, s2, half) * scale).T.astype(BF16)
    for h in range(GQA_KV_HEADS):
        sl = slice(h * LANES, (h + 1) * LANES)
        x = _rms(k_in[0, :, sl].astype(F32), kn_ref[...])
        k_ref[0, :, sl] = _rope(x, c, s1, s2, half).astype(BF16)
    _store_vt(vt_ref, v_in[0].astype(F32).T, v_in.shape[1])


def _gqa_prep(hx, tabs, qn, kn):
    bsz, s_len, _ = hx.shape
    tm = ROW_TILE
    qw = GQA_HEADS * GQA_DH
    kw = GQA_KV_HEADS * GQA_DH
    tab = pl.BlockSpec((tm, LANES), lambda b, i: (i, 0))
    kern = functools.partial(_gqa_prep_kernel, scale=LOG2E * GQA_DH ** -0.5)
    return pl.pallas_call(
        kern,
        grid=(bsz, s_len // tm),
        in_specs=[pl.BlockSpec((1, tm, qw), lambda b, i: (b, i, 0)),
                  pl.BlockSpec((1, tm, kw), lambda b, i: (b, i, qw // kw)),
                  pl.BlockSpec((1, tm, kw), lambda b, i: (b, i, qw // kw + 1)),
                  tab, tab, tab, _const_spec((1, GQA_DH)), _const_spec((1, GQA_DH))],
        out_specs=[pl.BlockSpec((1, qw, tm), lambda b, i: (b, 0, i)),
                   pl.BlockSpec((1, tm, kw), lambda b, i: (b, i, 0)),
                   pl.BlockSpec((1, kw // LANES * VT_ROWS, tm), lambda b, i: (b, 0, i))],
        out_shape=[jax.ShapeDtypeStruct((bsz, qw, s_len), BF16),
                   jax.ShapeDtypeStruct((bsz, s_len, kw), BF16),
                   jax.ShapeDtypeStruct((bsz, kw // LANES * VT_ROWS, s_len), BF16)],
        compiler_params=_params(("parallel", "parallel")),
        name="gqa_prep",
    )(hx, hx, hx, *tabs, qn.reshape(1, -1), kn.reshape(1, -1))


def _attn_kernel(qt_ref, k_ref, vt_ref, o_ref, m_sc, l_sc, acc_sc, *,
                 n_heads, k_shared, v_pair, latent_queries, tq, tk, t_len, s_len):
    def scores(g, kc):
        kg = kc if k_shared else kc[:, g * LANES:(g + 1) * LANES]
        return jnp.dot(kg, qt_ref[0, g * LANES:(g + 1) * LANES, :], preferred_element_type=F32)

    def k_chunk(j):
        return k_ref[0, pl.ds(pl.multiple_of(j * tk, tk), tk), :]

    def update(g, st, vte, first):
        m_new = jnp.max(st, axis=0, keepdims=True)
        if not first:
            m_old = m_sc[g][0:1]
            m_new = jnp.maximum(m_old, m_new)
            alpha = jnp.exp2(m_old - m_new)
        p = jnp.exp2(st - m_new).astype(BF16)
        res = jnp.dot(vte, p, preferred_element_type=F32)
        if first:
            acc_sc[g] = res[:LANES]
            l_sc[g] = res[LANES:LANES + 8]
        else:
            acc_sc[g] = alpha * acc_sc[g] + res[:LANES]
            l_sc[g] = alpha * l_sc[g] + res[LANES:LANES + 8]
        m_sc[g] = jnp.broadcast_to(m_new, (8, tq))

    kc = k_ref[0, t_len:s_len, :]
    vte = vt_ref[0, :, t_len:s_len]
    st_next = scores(0, kc)
    for g in range(n_heads):
        st = st_next
        if g + 1 < n_heads:
            st_next = scores(g + 1, kc)
        elif latent_queries:
            st_next = scores(0, k_chunk(0))
        update(g, st, vte, True)

    if latent_queries:
        n_chunks = t_len // tk

        def body(j, st_next):
            kc = k_chunk(j)
            vte = vt_ref[0, :, pl.ds(pl.multiple_of(j * tk, tk), tk)]
            for g in range(n_heads):
                st = st_next
                if g + 1 < n_heads:
                    st_next = scores(g + 1, kc)
                else:
                    st_next = scores(0, k_chunk(jnp.minimum(j + 1, n_chunks - 1)))
                update(g, st, vte, False)
            return st_next

        lax.fori_loop(0, n_chunks, body, st_next, unroll=ATT_UNROLL)

    outs = [acc_sc[g] / l_sc[g][0:1] for g in range(n_heads)]
    if v_pair:
        row = lax.broadcasted_iota(jnp.int32, (LANES, tq), 0)
        o_ref[0] = jnp.where(row < LANES // 2, outs[0], outs[1]).T.astype(o_ref.dtype)
    else:
        for g in range(n_heads):
            o_ref[0, :, g * LANES:(g + 1) * LANES] = outs[g].T.astype(o_ref.dtype)


def _attention(qt, k, vt, *, n_groups, n_heads, k_shared, v_pair, t_len, latent_queries):
    bsz, s_len, _ = k.shape
    tq = ATT_TQ if latent_queries else s_len - t_len
    n_q_rows = t_len if latent_queries else s_len - t_len
    q0 = 0 if latent_queries else t_len // tq
    tk = ATT_TK
    qw = n_heads * LANES
    kw = LANES if k_shared else qw
    ow = LANES if v_pair else qw
    kern = functools.partial(_attn_kernel, n_heads=n_heads, k_shared=k_shared, v_pair=v_pair,
                             latent_queries=latent_queries, tq=tq, tk=tk, t_len=t_len, s_len=s_len)
    return pl.pallas_call(
        kern,
        grid=(bsz, n_groups, n_q_rows // tq),
        in_specs=[pl.BlockSpec((1, qw, tq), lambda b, g, i: (b, g, q0 + i)),
                  pl.BlockSpec((1, s_len, kw), lambda b, g, i: (b, 0, g)),
                  pl.BlockSpec((1, VT_ROWS, s_len), lambda b, g, i: (b, g, 0))],
        out_specs=pl.BlockSpec((1, tq, ow), lambda b, g, i: (b, i, g)),
        out_shape=jax.ShapeDtypeStruct((bsz, n_q_rows, n_groups * ow), BF16),
        scratch_shapes=[pltpu.VMEM((n_heads, 8, tq), F32),
                        pltpu.VMEM((n_heads, 8, tq), F32),
                        pltpu.VMEM((n_heads, LANES, tq), F32)],
        compiler_params=_params(("parallel", "parallel", "arbitrary")),
        name="attention",
    )(qt, k, vt)


def _conv_kernel(cur_ref, prev_ref, next_ref, w_ref, o_ref, *, tm, t_len, s_len, qscale):
    i = pl.program_id(1)
    x = cur_ref[0].astype(F32)
    row = i * tm + lax.broadcasted_iota(jnp.int32, (tm, 1), 0)
    local = lax.broadcasted_iota(jnp.int32, (tm, 1), 0)
    prev_row = prev_ref[0, 15:16, :].astype(F32)
    next_row = next_ref[0, 0:1, :].astype(F32)
    xm1 = jnp.where(local == 0, prev_row, pltpu.roll(x, 1, 0))
    xp1 = jnp.where(local == tm - 1, next_row, pltpu.roll(x, tm - 1, 0))
    xm1 = jnp.where((row == 0) | (row == t_len), 0.0, xm1)
    xp1 = jnp.where((row == t_len - 1) | (row == s_len - 1), 0.0, xp1)
    y = w_ref[0:1, :] * xm1 + w_ref[1:2, :] * x + w_ref[2:3, :] * xp1
    y = y * jax.nn.sigmoid(y)
    half = y.shape[1] // 2
    o_ref[0, :, :half] = (y[:, :half] * qscale).astype(BF16)
    o_ref[0, :, half:] = y[:, half:].astype(BF16)


def _conv_silu(qk_pre, conv_w, t_len):
    bsz, s_len, w = qk_pre.shape
    tm = ROW_TILE
    r16 = tm // 16
    n16 = s_len // 16
    kern = functools.partial(_conv_kernel, tm=tm, t_len=t_len, s_len=s_len, qscale=ML_DK ** -0.5)
    return pl.pallas_call(
        kern,
        grid=(bsz, s_len // tm),
        in_specs=[pl.BlockSpec((1, tm, w), lambda b, i: (b, i, 0)),
                  pl.BlockSpec((1, 16, w), lambda b, i: (b, jnp.maximum(i * r16 - 1, 0), 0)),
                  pl.BlockSpec((1, 16, w), lambda b, i: (b, jnp.minimum((i + 1) * r16, n16 - 1), 0)),
                  _const_spec((8, w))],
        out_specs=pl.BlockSpec((1, tm, w), lambda b, i: (b, i, 0)),
        out_shape=jax.ShapeDtypeStruct((bsz, s_len, w), BF16),
        compiler_params=_params(("parallel", "parallel")),
        name="conv_silu",
    )(qk_pre, qk_pre, qk_pre, jnp.pad(conv_w, ((0, 8 - conv_w.shape[0]), (0, 0))))


def _split3(a):
    a1 = a.astype(BF16)
    r1 = a - a1.astype(F32)
    a2 = r1.astype(BF16)
    a3 = (r1 - a2.astype(F32)).astype(BF16)
    return a1, a2, a3


def _mlstm_kernel(qkf_ref, qkb_ref, vf_ref, vb_ref, gf_ref, gb_ref, bias_ref,
                  hf_ref, hb_ref, c_sc, m_sc):
    L = ML_CHUNK

    @pl.when(pl.program_id(1) == 0)
    def _():
        c_sc[...] = jnp.zeros_like(c_sc)
        m_sc[...] = jnp.zeros_like(m_sc)

    r = lax.broadcasted_iota(jnp.int32, (L, L), 0)
    c = lax.broadcasted_iota(jnp.int32, (L, L), 1)
    lane = lax.broadcasted_iota(jnp.int32, (1, LANES), 1) - GATE_OFF
    is_forget = (lane >= 0) & (lane < N_GATES) & ((lane % (2 * ML_HEADS)) >= ML_HEADS)
    ones = jnp.ones((L, ML_DV), BF16)
    dirs = ((qkf_ref, vf_ref, gf_ref, hf_ref), (qkb_ref, vb_ref, gb_ref, hb_ref))

    gate_terms, early = [], {}
    for d, (qk_ref, v_ref, g_ref, h_ref) in enumerate(dirs):
        seen = (c <= r) if d == 0 else (c >= r)
        seen_b = seen.astype(F32).astype(BF16)
        gates = g_ref[0] + bias_ref[...]
        soft = jnp.minimum(gates, 0.0) - jnp.log1p(jnp.exp(-jnp.abs(gates)))
        logs = jnp.where(is_forget, soft, gates)
        logs_t = logs.T[GATE_OFF:GATE_OFF + N_GATES, :]
        cum_col = sum(jnp.dot(seen_b, p, preferred_element_type=F32) for p in _split3(logs))
        cum_row = sum(lax.dot_general(p, seen_b, (((1,), (1,)), ((), ())), preferred_element_type=F32)
                      for p in _split3(logs_t))
        gate_terms.append((seen, logs, logs_t, cum_col, cum_row))
        for h in range(ML_HEADS):
            sl = slice(h * LANES, (h + 1) * LANES)
            q = qk_ref[0, :, sl]
            k = qk_ref[0, :, ML_HEADS * ML_DK + h * LANES:ML_HEADS * ML_DK + (h + 1) * LANES]
            state = c_sc[d, h]
            qk = lax.dot_general(q, k, (((1,), (1,)), ((), ())), preferred_element_type=F32)
            inter = jnp.dot(q, state.astype(BF16), preferred_element_type=F32)
            early[d, h] = (k, state, qk, inter)

    for d, (qk_ref, v_ref, g_ref, h_ref) in enumerate(dirs):
        seen, logs, logs_t, cum_col, cum_row = gate_terms[d]
        last = L - 1 if d == 0 else 0
        for h in range(ML_HEADS):
            gi = d * 2 * ML_HEADS + h
            gf = gi + ML_HEADS
            b_col = cum_col[:, GATE_OFF + gf:GATE_OFF + gf + 1]
            b_row = cum_row[gf:gf + 1, :]
            li_col = logs[:, GATE_OFF + gi:GATE_OFF + gi + 1]
            li_row = logs_t[gi:gi + 1, :]
            b_end = b_row[:, last:last + 1]
            m_prev = m_sc[d, h][0:1, 0:1]
            sl = slice(h * LANES, (h + 1) * LANES)
            k, state, qk, inter = early[d, h]
            vext = jnp.concatenate([v_ref[0, :, sl], ones], axis=-1)

            b_all = jnp.broadcast_to(b_col, (L, L))
            logw = jnp.where(seen, b_all - b_row + li_row, -jnp.inf)
            m_t = jnp.broadcast_to(
                jnp.maximum(b_col + m_prev, jnp.max(logw, axis=-1, keepdims=True)), (L, L))
            w_state = jnp.exp(b_all + m_prev - m_t)
            s = qk * jnp.exp(logw - m_t)
            intra = jnp.dot(s.astype(BF16), vext, preferred_element_type=F32)
            num = w_state * inter[:, :ML_DV] + intra[:, :ML_DV]
            den = w_state * inter[:, ML_DV:] + intra[:, ML_DV:]
            h_ref[0, :, sl] = num / jnp.maximum(jnp.abs(den), jnp.exp(-m_t))

            g_all = b_end - b_all + jnp.broadcast_to(li_col, (L, L))
            m_new = jnp.maximum(b_end + m_prev, jnp.max(g_all, axis=0, keepdims=True)[:, 0:1])
            decay = jnp.exp(b_end + m_prev - m_new)
            wk = (jnp.exp(g_all - m_new) * k.astype(F32)).astype(BF16)
            upd = lax.dot_general(wk, vext, (((0,), (0,)), ((), ())), preferred_element_type=F32)
            c_sc[d, h] = decay * state + upd
            m_sc[d, h] = jnp.broadcast_to(m_new, (8, LANES))


def _mlstm(qk, v, side, gate_bias_row, t_len):
    bsz, s_len, _ = qk.shape
    L = ML_CHUNK
    n_chunks = s_len // L
    n_lat = t_len // L
    hw = ML_HEADS * ML_DV
    fwd = lambda b, j: (b, (j + n_lat) % n_chunks, 0)
    bwd = lambda b, j: (b, n_chunks - 1 - j, 0)
    return pl.pallas_call(
        _mlstm_kernel,
        grid=(bsz, n_chunks),
        in_specs=[pl.BlockSpec((1, L, 2 * hw), fwd), pl.BlockSpec((1, L, 2 * hw), bwd),
                  pl.BlockSpec((1, L, hw), fwd), pl.BlockSpec((1, L, hw), bwd),
                  pl.BlockSpec((1, L, LANES), fwd), pl.BlockSpec((1, L, LANES), bwd),
                  _const_spec((1, LANES))],
        out_specs=[pl.BlockSpec((1, L, hw), fwd), pl.BlockSpec((1, L, hw), bwd)],
        out_shape=[jax.ShapeDtypeStruct((bsz, s_len, hw), F32)] * 2,
        scratch_shapes=[pltpu.VMEM((2, ML_HEADS, ML_DK, 2 * ML_DV), F32),
                        pltpu.VMEM((2, ML_HEADS, 8, LANES), F32)],
        compiler_params=_params(("parallel", "arbitrary")),
        name="mlstm",
    )(qk, qk, v, v, side, side, gate_bias_row)


def _finish_kernel(*refs, n_stream, tm, t_len, with_readout, with_final, ff_chunk):
    h_refs = refs[:n_stream]
    it = iter(refs[n_stream:])
    if with_readout:
        alat_ref, actx_ref, hf_ref, hb_ref, o_ref, hn_ref = (next(it) for _ in range(6))
    else:
        mix_ref = next(it)
    mx_ref, mc_ref, wo_ref, n2_ref, w1_ref, w2_ref = (next(it) for _ in range(6))
    fn_ref = next(it) if with_final else None
    out_ref = next(it)

    is_ctx = _row_is_ctx(pl.program_id(1), tm, t_len)
    mod = lambda j: jnp.where(is_ctx, mc_ref[j:j + 1, :], mx_ref[0, j:j + 1, :])

    if with_readout:
        hsum = hf_ref[0] + hb_ref[0]
        parts = [_rms(hsum[:, h * ML_DV:(h + 1) * ML_DV], hn_ref[:, h * ML_DV:(h + 1) * ML_DV])
                 for h in range(ML_HEADS)]
        rec = jax.nn.sigmoid(o_ref[0].astype(F32)) * jnp.concatenate(parts, axis=-1)
        att = jnp.where(is_ctx, actx_ref[0], alat_ref[0])
        mix = jnp.concatenate([att, rec.astype(BF16)], axis=-1)
    else:
        mix = mix_ref[0]
    s = _stream_tile(h_refs, is_ctx) + mod(2) * jnp.dot(mix, wo_ref[...], preferred_element_type=F32)
    y = (_rms(s, n2_ref[...]) * (1.0 + mod(4)) + mod(3)).astype(BF16)
    d_ff = w1_ref.shape[1]
    acc = jnp.zeros(s.shape, F32)
    for c0 in range(0, d_ff, ff_chunk):
        u = jnp.maximum(jnp.dot(y, w1_ref[:, c0:c0 + ff_chunk], preferred_element_type=F32), 0.0)
        acc = acc + jnp.dot((u * u).astype(BF16), w2_ref[c0:c0 + ff_chunk, :],
                            preferred_element_type=F32)
    out = s + mod(5) * acc
    if with_final:
        out = _rms(out, fn_ref[...])
    out_ref[0] = out


def _finish(hs, mix_parts, modsx, modsc, w_out, norm2, w1, w2, *, t_len, n_rows, head_norm=None,
            final_norm=None):
    bsz, _, d = hs[0].shape
    tm = ROW_TILE
    ns = len(hs)
    with_readout = head_norm is not None
    with_final = final_norm is not None
    tok = lambda w: pl.BlockSpec((1, tm, w), lambda b, i: (b, i, 0))
    once = lambda shape: pl.BlockSpec(shape, lambda *_: (0,) * len(shape),
                                      pipeline_mode=pl.Buffered(1))
    args = list(hs) + list(mix_parts)
    specs = _stream_specs(hs, tm, t_len) + [tok(p.shape[-1]) for p in mix_parts]
    if with_readout:
        n_lat = t_len // tm
        aw = mix_parts[0].shape[-1]
        specs[ns] =pl.BlockSpec((1, tm, aw), lambda b, i: (b, jnp.minimum(i, n_lat - 1), 0))
        specs[ns + 1] =pl.BlockSpec((1, tm, aw), lambda b, i: (b, jnp.maximum(i - n_lat, 0), 0))
        args.append(head_norm.reshape(1, -1))
        specs.append(_const_spec((1, head_norm.shape[0])))
    args += [modsx, modsc, w_out, norm2.reshape(1, d), w1, w2]
    specs += [pl.BlockSpec((1, 8, d), lambda b, i: (b, 0, 0)), _const_spec((8, d)),
              once(w_out.shape), _const_spec((1, d)), once(w1.shape), once(w2.shape)]
    if with_final:
        args.append(final_norm.reshape(1, d))
        specs.append(_const_spec((1, d)))
    kern = functools.partial(_finish_kernel, n_stream=ns, tm=tm, t_len=t_len, with_readout=with_readout,
                             with_final=with_final, ff_chunk=1024)
    return pl.pallas_call(
        kern,
        grid=(bsz, n_rows // tm),
        in_specs=specs,
        out_specs=tok(d),
        out_shape=jax.ShapeDtypeStruct((bsz, n_rows, d), F32),
        compiler_params=_params(("parallel", "parallel")),
        name="finish",
    )(*args)


def _rope_tables(t_len, ctx_len, d_rot, lane0):
    n_freq = d_rot // 4
    n = 2 * n_freq
    rel = np.arange(LANES) - lane0
    first = (rel >= 0) & (rel < n)
    second = (rel >= n) & (rel < 2 * n)
    idx = np.clip(np.where(second, rel - n, rel), 0, n - 1)
    freqs = ROPE_THETA ** (-jnp.arange(n_freq, dtype=F32) / n_freq)
    t = jnp.arange(t_len + ctx_len)[:, None]
    pos = jnp.where(jnp.asarray(idx // n_freq == 0)[None, :], t // GRID_W, t % GRID_W).astype(F32)
    ang = pos * freqs[idx % n_freq][None, :]
    live = t < t_len
    cos, sin = jnp.cos(ang), jnp.sin(ang)
    c = jnp.where(live & jnp.asarray(first | second)[None, :], cos, 1.0)
    s1 = jnp.where(live & jnp.asarray(second)[None, :], sin, 0.0)
    s2 = jnp.where(live & jnp.asarray(first)[None, :], -sin, 0.0)
    return c, s1, s2


def _mla_rope_cols():
    nf = MLA_ROPE // 4
    first = [a * 2 * nf + f for a in range(2) for f in range(nf)]
    second = [a * 2 * nf + nf + f for a in range(2) for f in range(nf)]
    return np.array(first + second)


def _gqa_head_cols():
    nf = GQA_DH // 4
    first = [a * 2 * nf + f for a in range(2) for f in range(nf)]
    second = [a * 2 * nf + nf + f for a in range(2) for f in range(nf)]
    return np.array(first + second)


def kernel(x, c, ctx, c_ctx,
           l0_ada_w, l0_ada_b, l0_norm1, l0_w_in, l0_mla_q_norm, l0_mla_w_uq, l0_mla_kv_norm, l0_mla_w_ukv,
           l0_ml_conv, l0_ml_gate_b, l0_ml_head_norm, l0_w_out, l0_norm2, l0_w1, l0_w2,
           l1_ada_w, l1_ada_b, l1_norm1, l1_w_in, l1_q_norm, l1_k_norm, l1_w_out, l1_norm2, l1_w1, l1_w2,
           final_norm):
    bsz, t_len, d = x.shape
    ctx_len = ctx.shape[1]
    s_len = t_len + ctx_len

    cv = jnp.zeros((8, d), F32).at[:bsz].set(c).at[bsz].set(c_ctx)

    def mods(ada_w, ada_b):
        m = _adaln(cv, ada_w, ada_b).reshape(8, N_MOD, d)
        pad = jnp.zeros((8 - N_MOD, d), F32)
        modsx = jnp.concatenate([m[:bsz], jnp.broadcast_to(pad, (bsz,) + pad.shape)], axis=1)
        modsc = jnp.concatenate([m[bsz], pad], axis=0)
        return modsx, modsc

    modsx, modsc = mods(l0_ada_w, l0_ada_b)
    o = np.cumsum([0, MLA_Q_RANK, MLA_KV_RANK, MLA_ROPE, 2 * ML_HEADS * ML_DK, ML_HEADS * ML_DV,
                   ML_HEADS * ML_DV, N_GATES])
    w_cq, w_ckv, w_kpe, w_qk, w_v, w_o, w_g = (l0_w_in[:, o[i]:o[i + 1]] for i in range(7))
    side_pad = jnp.zeros((d, LANES - MLA_ROPE - N_GATES), F32)
    w_in0 = jnp.concatenate([w_qk, w_v, w_o, w_cq, w_ckv, w_kpe[:, _mla_rope_cols()], w_g, side_pad],
                            axis=1).astype(BF16)
    widths0 = (2 * ML_HEADS * ML_DK, ML_HEADS * ML_DV, ML_HEADS * ML_DV, MLA_Q_RANK + MLA_KV_RANK, LANES)
    qk_pre, v_ml, o_pre, lat, side = _inproj((x, ctx), l0_norm1, modsx, modsc, w_in0, widths0,
                                             (BF16, BF16, BF16, BF16, F32), t_len, s_len)

    dq = MLA_NOPE + MLA_ROPE
    head_cols = np.concatenate([np.arange(MLA_NOPE), MLA_NOPE + _mla_rope_cols()])
    wq = l0_mla_w_uq.reshape(MLA_Q_RANK, MLA_HEADS, dq)[:, :, head_cols]
    wq = jnp.pad(wq, ((0, 0), (0, 0), (0, LANES - dq))).reshape(MLA_Q_RANK, MLA_HEADS * LANES).astype(BF16)
    wkv = l0_mla_w_ukv.reshape(MLA_KV_RANK, MLA_HEADS, MLA_NOPE + MLA_V)
    wk_nope = jnp.pad(wkv[:, :, :MLA_NOPE], ((0, 0), (0, 0), (0, LANES - MLA_NOPE)))
    place = np.zeros((LANES, MLA_HEADS, LANES), np.float32)
    for hh in range(MLA_HEADS):
        place[KPE_OFF + np.arange(MLA_ROPE), hh, MLA_NOPE + np.arange(MLA_ROPE)] = 1.0
    wk = jnp.concatenate([wk_nope, jnp.asarray(place)], axis=0).reshape(
        MLA_KV_RANK + LANES, MLA_HEADS * LANES).astype(BF16)
    wv = wkv[:, :, MLA_NOPE:].reshape(MLA_KV_RANK, MLA_HEADS * MLA_V).astype(BF16)
    tabs_mla = _rope_tables(t_len, ctx_len, MLA_ROPE, MLA_NOPE)
    q_mla, k_mla, v_mla = _mla_prep(lat, side, tabs_mla, l0_mla_q_norm, l0_mla_kv_norm, wq, wk, wv)
    mla_att = functools.partial(_attention, q_mla, k_mla, v_mla, n_groups=MLA_HEADS // 2, n_heads=2,
                                k_shared=False, v_pair=True, t_len=t_len)
    a_lat, a_ctx = mla_att(latent_queries=True), mla_att(latent_queries=False)

    qk = _conv_silu(qk_pre, l0_ml_conv, t_len)
    bias_row = jnp.zeros((1, LANES), F32).at[0, GATE_OFF:GATE_OFF + N_GATES].set(l0_ml_gate_b)
    hf, hb = _mlstm(qk, v_ml, side, bias_row, t_len)
    h = _finish((x, ctx), (a_lat, a_ctx, hf, hb, o_pre), modsx, modsc, l0_w_out.astype(BF16),
                l0_norm2, l0_w1.astype(BF16), l0_w2.astype(BF16), t_len=t_len, n_rows=s_len,
                head_norm=l0_ml_head_norm)

    modsx, modsc = mods(l1_ada_w, l1_ada_b)
    hc = _gqa_head_cols()
    n_qk = GQA_HEADS + GQA_KV_HEADS
    qk_cols = (np.arange(n_qk)[:, None] * GQA_DH + hc[None, :]).reshape(-1)
    w_in1 = jnp.concatenate([l1_w_in[:, qk_cols], l1_w_in[:, n_qk * GQA_DH:]], axis=1).astype(BF16)
    (hx,) = _inproj((h,), l1_norm1, modsx, modsc, w_in1, (w_in1.shape[1],), (BF16,), t_len, s_len)
    tabs_gqa = _rope_tables(t_len, ctx_len, GQA_DH, 0)
    q_gqa, k_gqa, v_gqa = _gqa_prep(hx, tabs_gqa, l1_q_norm[hc], l1_k_norm[hc])
    att = _attention(q_gqa, k_gqa, v_gqa, n_groups=GQA_KV_HEADS, n_heads=GQA_GROUP, k_shared=True,
                     v_pair=False, t_len=t_len, latent_queries=True)
    return _finish((h,), (att,), modsx, modsc, l1_w_out.astype(BF16), l1_norm2,
                   l1_w1.astype(BF16), l1_w2.astype(BF16), t_len=t_len, n_rows=t_len,
                   final_norm=final_norm)
```

```python
import functools

import numpy as np
import jax
import jax.numpy as jnp
from jax import lax
from jax.experimental import pallas as pl
from jax.experimental.pallas import tpu as pltpu

F32 = jnp.float32
BF16 = jnp.bfloat16

GRID_W = 64
NORM_EPS = 1e-6
ROPE_THETA = 10000.0
N_MOD = 6
MLA_HEADS = 8
MLA_Q_RANK = 256
MLA_KV_RANK = 128
MLA_NOPE = 64
MLA_ROPE = 32
MLA_V = 64
ML_HEADS = 4
ML_DK = 128
ML_DV = 128
ML_CHUNK = 128
N_GATES = 4 * ML_HEADS
GQA_HEADS = 8
GQA_KV_HEADS = 2
GQA_GROUP = GQA_HEADS // GQA_KV_HEADS
GQA_DH = 128

LANES = 128
V7X_VMEM_BYTES = 64 * 1024 * 1024
VMEM_LIMIT = V7X_VMEM_BYTES * 7 // 8

ROW_TILE = 256
ATT_TQ = 1024
ATT_TK = 256
ATT_UNROLL = 8
KPE_OFF = 0
GATE_OFF = 32


def _params(sem):
    return pltpu.CompilerParams(dimension_semantics=sem, vmem_limit_bytes=VMEM_LIMIT)


def _rms(x, g):
    var = jnp.mean(x * x, axis=-1, keepdims=True)
    return x * lax.rsqrt(var + NORM_EPS) * g


def _const_spec(shape):
    nd = len(shape)
    return pl.BlockSpec(shape, lambda *_: (0,) * nd)


def _adaln_kernel(c_ref, w_ref, b_ref, o_ref):
    c = c_ref[...]
    a = c * jax.nn.sigmoid(c)
    o_ref[...] = jnp.dot(a.astype(BF16), w_ref[...].astype(BF16),
                         preferred_element_type=F32) + b_ref[...]


def _adaln(cv, w, b):
    d = cv.shape[1]
    n = w.shape[1]
    tn = n // 4
    return pl.pallas_call(
        _adaln_kernel,
        grid=(n // tn,),
        in_specs=[pl.BlockSpec((8, d), lambda j: (0, 0)),
                  pl.BlockSpec((d, tn), lambda j: (0, j)),
                  pl.BlockSpec((1, tn), lambda j: (0, j))],
        out_specs=pl.BlockSpec((8, tn), lambda j: (0, j)),
        out_shape=jax.ShapeDtypeStruct((8, n), F32),
        compiler_params=_params(("arbitrary",)),
        name="adaln",
    )(cv, w, b.reshape(1, n))


def _row_is_ctx(tile, tm, t_len):
    row = tile * tm + lax.broadcasted_iota(jnp.int32, (tm, 1), 0)
    return row >= t_len


def _stream_specs(hs, tm, t_len):
    d = hs[0].shape[-1]
    if len(hs) == 1:
        return [pl.BlockSpec((1, tm, d), lambda b, i: (b, i, 0))]
    n_lat = t_len // tm
    return [pl.BlockSpec((1, tm, d), lambda b, i: (b, jnp.minimum(i, n_lat - 1), 0)),
            pl.BlockSpec((1, tm, d), lambda b, i: (b, jnp.maximum(i - n_lat, 0), 0))]


def _stream_tile(h_refs, is_ctx):
    if len(h_refs) == 1:
        return h_refs[0][0]
    return jnp.where(is_ctx, h_refs[1][0], h_refs[0][0])


def _inproj_kernel(*refs, n_stream, tm, t_len, widths):
    h_refs, (g_ref, mx_ref, mc_ref, w_ref), out_refs = (
        refs[:n_stream], refs[n_stream:n_stream + 4], refs[n_stream + 4:])
    is_ctx = _row_is_ctx(pl.program_id(1), tm, t_len)
    shift = jnp.where(is_ctx, mc_ref[0:1, :], mx_ref[0, 0:1, :])
    scale = jnp.where(is_ctx, mc_ref[1:2, :], mx_ref[0, 1:2, :])
    y = (_rms(_stream_tile(h_refs, is_ctx), g_ref[...]) * (1.0 + scale) + shift).astype(BF16)
    off = 0
    for o_ref, width in zip(out_refs, widths):
        o_ref[0] = jnp.dot(y, w_ref[:, off:off + width],
                           preferred_element_type=F32).astype(o_ref.dtype)
        off += width


def _inproj(hs, g, modsx, modsc, w, widths, dtypes, t_len, s_len):
    bsz, _, d = hs[0].shape
    tm = ROW_TILE
    kern = functools.partial(_inproj_kernel, n_stream=len(hs), tm=tm, t_len=t_len, widths=widths)
    return pl.pallas_call(
        kern,
        grid=(bsz, s_len // tm),
        in_specs=_stream_specs(hs, tm, t_len) + [
            _const_spec((1, d)),
            pl.BlockSpec((1, 8, d), lambda b, i: (b, 0, 0)),
            _const_spec((8, d)),
            _const_spec(w.shape)],
        out_specs=[pl.BlockSpec((1, tm, wd), lambda b, i: (b, i, 0)) for wd in widths],
        out_shape=[jax.ShapeDtypeStruct((bsz, s_len, wd), dt) for wd, dt in zip(widths, dtypes)],
        compiler_params=_params(("parallel", "parallel")),
        name="inproj",
    )(*hs, g.reshape(1, d), modsx, modsc, w)


def _rope(x, c, s1, s2, half):
    return x * c + pltpu.roll(x, half, 1) * s1 + pltpu.roll(x, LANES - half, 1) * s2


LOG2E = 1.4426950408889634
BF16_SUBLANES = 16
VT_ROWS = LANES + BF16_SUBLANES


def _store_vt(vt_ref, vt, tm):
    for g in range(vt.shape[0] // LANES):
        vt_ref[0, g * VT_ROWS:g * VT_ROWS + LANES, :] = vt[g * LANES:(g + 1) * LANES].astype(BF16)
        vt_ref[0, g * VT_ROWS + LANES:(g + 1) * VT_ROWS, :] = jnp.ones((BF16_SUBLANES, tm), BF16)


def _mla_prep_kernel(lat_ref, side_ref, c_ref, s1_ref, s2_ref, qn_ref, kvn_ref,
                     wq_ref, wk_ref, wv_ref, qt_ref, k_ref, vt_ref, *, scale):
    lat = lat_ref[0].astype(F32)
    cqn = _rms(lat[:, :MLA_Q_RANK], qn_ref[...]).astype(BF16)
    ckvn = _rms(lat[:, MLA_Q_RANK:], kvn_ref[...]).astype(BF16)
    c, s1, s2 = c_ref[...], s1_ref[...], s2_ref[...]
    q = jnp.dot(cqn, wq_ref[...], preferred_element_type=F32)
    kin = jnp.concatenate([ckvn, side_ref[0].astype(BF16)], axis=-1)
    k = jnp.dot(kin, wk_ref[...], preferred_element_type=F32)
    half = MLA_ROPE // 2
    for h in range(MLA_HEADS):
        sl = slice(h * LANES, (h + 1) * LANES)
        qt_ref[0, sl, :] = (_rope(q[:, sl], c, s1, s2, half) * scale).T.astype(BF16)
        k_ref[0, :, sl] = _rope(k[:, sl], c, s1, s2, half).astype(BF16)
    _store_vt(vt_ref, jnp.dot(ckvn, wv_ref[...], preferred_element_type=F32).T, lat.shape[0])


def _mla_prep(lat, side, tabs, qn, kvn, wq, wk, wv):
    bsz, s_len, lw = lat.shape
    tm = ROW_TILE
    hw = MLA_HEADS * LANES
    vw = MLA_HEADS * MLA_V
    tok = lambda w: pl.BlockSpec((1, tm, w), lambda b, i: (b, i, 0))
    tok_t = lambda w: pl.BlockSpec((1, w, tm), lambda b, i: (b, 0, i))
    tab = pl.BlockSpec((tm, LANES), lambda b, i: (i, 0))
    kern = functools.partial(_mla_prep_kernel, scale=LOG2E * (MLA_NOPE + MLA_ROPE) ** -0.5)
    return pl.pallas_call(
        kern,
        grid=(bsz, s_len // tm),
        in_specs=[tok(lw), tok(LANES), tab, tab, tab,
                  _const_spec((1, MLA_Q_RANK)), _const_spec((1, MLA_KV_RANK)),
                  _const_spec(wq.shape), _const_spec(wk.shape), _const_spec(wv.shape)],
        out_specs=[tok_t(hw), tok(hw), tok_t(vw // LANES * VT_ROWS)],
        out_shape=[jax.ShapeDtypeStruct((bsz, hw, s_len), BF16),
                   jax.ShapeDtypeStruct((bsz, s_len, hw), BF16),
                   jax.ShapeDtypeStruct((bsz, vw // LANES * VT_ROWS, s_len), BF16)],
        compiler_params=_params(("parallel", "parallel")),
        name="mla_prep",
    )(lat, side, *tabs, qn.reshape(1, -1), kvn.reshape(1, -1), wq, wk, wv)


def _gqa_inproj_kernel(h_ref, g_ref, mx_ref, mc_ref, wqv_ref, wk_ref, ct_ref, st_ref,
                       c_ref, s1_ref, s2_ref, qnb_ref, kn_ref, qt_ref, k_ref, vt_ref,
                       *, tm, t_len, scale):
    is_ctx = _row_is_ctx(pl.program_id(1), tm, t_len)
    shift = jnp.where(is_ctx, mc_ref[0:1, :], mx_ref[0, 0:1, :])
    gain = jnp.where(is_ctx, mc_ref[1:2, :], mx_ref[0, 1:2, :])
    y = (_rms(h_ref[0], g_ref[...]) * (1.0 + gain) + shift).astype(BF16)
    qv = lax.dot_general(wqv_ref[...], y, (((1,), (1,)), ((), ())), preferred_element_type=F32)
    half = GQA_DH // 2
    ct, st = ct_ref[...], st_ref[...]
    qnb = qnb_ref[...] * scale
    for h in range(GQA_HEADS):
        x = qv[h * GQA_DH:(h + 1) * GQA_DH]
        xn = x * lax.rsqrt(jnp.mean(x * x, axis=0, keepdims=True) + NORM_EPS) * qnb
        a, b = xn[:half], xn[half:]
        qt_ref[0, h * GQA_DH:h * GQA_DH + half, :] = (a * ct - b * st).astype(BF16)
        qt_ref[0, h * GQA_DH + half:(h + 1) * GQA_DH, :] = (b * ct + a * st).astype(BF16)
    _store_vt(vt_ref, qv[GQA_HEADS * GQA_DH:], tm)
    k = jnp.dot(y, wk_ref[...], preferred_element_type=F32)
    c, s1, s2 = c_ref[...], s1_ref[...], s2_ref[...]
    for h in range(GQA_KV_HEADS):
        sl = slice(h * LANES, (h + 1) * LANES)
        k_ref[0, :, sl] = _rope(_rms(k[:, sl], kn_ref[...]), c, s1, s2, half).astype(BF16)


def _gqa_inproj(h, g, modsx, modsc, wqv_t, wk, tabs, qn, kn, t_len):
    bsz, s_len, d = h.shape
    tm = ROW_TILE
    qw = GQA_HEADS * GQA_DH
    kw = GQA_KV_HEADS * GQA_DH
    half = GQA_DH // 2
    c, s1, s2 = tabs
    ct, st = c[:, :half].T, s1[:, half:].T
    qnb = jnp.broadcast_to(qn[:, None], (GQA_DH, tm))
    tab = pl.BlockSpec((tm, LANES), lambda b, i: (i, 0))
    tab_t = pl.BlockSpec((half, tm), lambda b, i: (0, i))
    kern = functools.partial(_gqa_inproj_kernel, tm=tm, t_len=t_len, scale=LOG2E * GQA_DH ** -0.5)
    return pl.pallas_call(
        kern,
        grid=(bsz, s_len // tm),
        in_specs=[pl.BlockSpec((1, tm, d), lambda b, i: (b, i, 0)),
                  _const_spec((1, d)),
                  pl.BlockSpec((1, 8, d), lambda b, i: (b, 0, 0)),
                  _const_spec((8, d)),
                  _const_spec(wqv_t.shape), _const_spec(wk.shape),
                  tab_t, tab_t, tab, tab, tab,
                  _const_spec((GQA_DH, tm)), _const_spec((1, GQA_DH))],
        out_specs=[pl.BlockSpec((1, qw, tm), lambda b, i: (b, 0, i)),
                   pl.BlockSpec((1, tm, kw), lambda b, i: (b, i, 0)),
                   pl.BlockSpec((1, kw // LANES * VT_ROWS, tm), lambda b, i: (b, 0, i))],
        out_shape=[jax.ShapeDtypeStruct((bsz, qw, s_len), BF16),
                   jax.ShapeDtypeStruct((bsz, s_len, kw), BF16),
                   jax.ShapeDtypeStruct((bsz, kw // LANES * VT_ROWS, s_len), BF16)],
        compiler_params=_params(("parallel", "parallel")),
        name="gqa_inproj",
    )(h, g.reshape(1, d), modsx, modsc, wqv_t, wk, ct, st, c, s1, s2, qnb, kn.reshape(1, -1))


def _attn_kernel(qt_ref, k_ref, vt_ref, o_ref, m_sc, l_sc, acc_sc, *,
                 n_heads, k_shared, v_pair, latent_queries, tq, tk, t_len, s_len):
    def scores(g, kc):
        kg = kc if k_shared else kc[:, g * LANES:(g + 1) * LANES]
        return jnp.dot(kg, qt_ref[0, g * LANES:(g + 1) * LANES, :], preferred_element_type=F32)

    def k_chunk(j):
        return k_ref[0, pl.ds(pl.multiple_of(j * tk, tk), tk), :]

    def update(g, st, vte, first):
        m_new = jnp.max(st, axis=0, keepdims=True)
        if not first:
            m_old = m_sc[g][0:1]
            m_new = jnp.maximum(m_old, m_new)
            alpha = jnp.exp2(m_old - m_new)
        p = jnp.exp2(st - m_new).astype(BF16)
        res = jnp.dot(vte, p, preferred_element_type=F32)
        if first:
            acc_sc[g] = res[:LANES]
            l_sc[g] = res[LANES:LANES + 8]
        else:
            acc_sc[g] = alpha * acc_sc[g] + res[:LANES]
            l_sc[g] = alpha * l_sc[g] + res[LANES:LANES + 8]
        m_sc[g] = jnp.broadcast_to(m_new, (8, tq))

    kc = k_ref[0, t_len:s_len, :]
    vte = vt_ref[0, :, t_len:s_len]
    st_next = scores(0, kc)
    for g in range(n_heads):
        st = st_next
        if g + 1 < n_heads:
            st_next = scores(g + 1, kc)
        elif latent_queries:
            st_next = scores(0, k_chunk(0))
        update(g, st, vte, True)

    if latent_queries:
        n_chunks = t_len // tk

        def body(j, st_next):
            kc = k_chunk(j)
            vte = vt_ref[0, :, pl.ds(pl.multiple_of(j * tk, tk), tk)]
            for g in range(n_heads):
                st = st_next
                if g + 1 < n_heads:
                    st_next = scores(g + 1, kc)
                else:
                    st_next = scores(0, k_chunk(jnp.minimum(j + 1, n_chunks - 1)))
                update(g, st, vte, False)
            return st_next

        lax.fori_loop(0, n_chunks, body, st_next, unroll=ATT_UNROLL)

    outs = [acc_sc[g] / l_sc[g][0:1] for g in range(n_heads)]
    if v_pair:
        row = lax.broadcasted_iota(jnp.int32, (LANES, tq), 0)
        o_ref[0] = jnp.where(row < LANES // 2, outs[0], outs[1]).T.astype(o_ref.dtype)
    else:
        for g in range(n_heads):
            o_ref[0, :, g * LANES:(g + 1) * LANES] = outs[g].T.astype(o_ref.dtype)


def _attention(qt, k, vt, *, n_groups, n_heads, k_shared, v_pair, t_len, latent_queries):
    bsz, s_len, _ = k.shape
    tq = ATT_TQ if latent_queries else s_len - t_len
    n_q_rows = t_len if latent_queries else s_len - t_len
    q0 = 0 if latent_queries else t_len // tq
    tk = ATT_TK
    qw = n_heads * LANES
    kw = LANES if k_shared else qw
    ow = LANES if v_pair else qw
    kern = functools.partial(_attn_kernel, n_heads=n_heads, k_shared=k_shared, v_pair=v_pair,
                             latent_queries=latent_queries, tq=tq, tk=tk, t_len=t_len, s_len=s_len)
    return pl.pallas_call(
        kern,
        grid=(bsz, n_groups, n_q_rows // tq),
        in_specs=[pl.BlockSpec((1, qw, tq), lambda b, g, i: (b, g, q0 + i)),
                  pl.BlockSpec((1, s_len, kw), lambda b, g, i: (b, 0, g)),
                  pl.BlockSpec((1, VT_ROWS, s_len), lambda b, g, i: (b, g, 0))],
        out_specs=pl.BlockSpec((1, tq, ow), lambda b, g, i: (b, i, g)),
        out_shape=jax.ShapeDtypeStruct((bsz, n_q_rows, n_groups * ow), BF16),
        scratch_shapes=[pltpu.VMEM((n_heads, 8, tq), F32),
                        pltpu.VMEM((n_heads, 8, tq), F32),
                        pltpu.VMEM((n_heads, LANES, tq), F32)],
        compiler_params=_params(("parallel", "parallel", "arbitrary")),
        name="attention",
    )(qt, k, vt)


def _conv_kernel(cur_ref, prev_ref, next_ref, w_ref, o_ref, *, tm, t_len, s_len, qscale):
    i = pl.program_id(1)
    x = cur_ref[0].astype(F32)
    row = i * tm + lax.broadcasted_iota(jnp.int32, (tm, 1), 0)
    local = lax.broadcasted_iota(jnp.int32, (tm, 1), 0)
    prev_row = prev_ref[0, 15:16, :].astype(F32)
    next_row = next_ref[0, 0:1, :].astype(F32)
    xm1 = jnp.where(local == 0, prev_row, pltpu.roll(x, 1, 0))
    xp1 = jnp.where(local == tm - 1, next_row, pltpu.roll(x, tm - 1, 0))
    xm1 = jnp.where((row == 0) | (row == t_len), 0.0, xm1)
    xp1 = jnp.where((row == t_len - 1) | (row == s_len - 1), 0.0, xp1)
    y = w_ref[0:1, :] * xm1 + w_ref[1:2, :] * x + w_ref[2:3, :] * xp1
    y = y * jax.nn.sigmoid(y)
    half = y.shape[1] // 2
    o_ref[0, :, :half] = (y[:, :half] * qscale).astype(BF16)
    o_ref[0, :, half:] = y[:, half:].astype(BF16)


def _conv_silu(qk_pre, conv_w, t_len):
    bsz, s_len, w = qk_pre.shape
    tm = ROW_TILE
    r16 = tm // 16
    n16 = s_len // 16
    kern = functools.partial(_conv_kernel, tm=tm, t_len=t_len, s_len=s_len, qscale=ML_DK ** -0.5)
    return pl.pallas_call(
        kern,
        grid=(bsz, s_len // tm),
        in_specs=[pl.BlockSpec((1, tm, w), lambda b, i: (b, i, 0)),
                  pl.BlockSpec((1, 16, w), lambda b, i: (b, jnp.maximum(i * r16 - 1, 0), 0)),
                  pl.BlockSpec((1, 16, w), lambda b, i: (b, jnp.minimum((i + 1) * r16, n16 - 1), 0)),
                  _const_spec((8, w))],
        out_specs=pl.BlockSpec((1, tm, w), lambda b, i: (b, i, 0)),
        out_shape=jax.ShapeDtypeStruct((bsz, s_len, w), BF16),
        compiler_params=_params(("parallel", "parallel")),
        name="conv_silu",
    )(qk_pre, qk_pre, qk_pre, jnp.pad(conv_w, ((0, 8 - conv_w.shape[0]), (0, 0))))


def _split3(a):
    a1 = a.astype(BF16)
    r1 = a - a1.astype(F32)
    a2 = r1.astype(BF16)
    a3 = (r1 - a2.astype(F32)).astype(BF16)
    return a1, a2, a3


def _mlstm_kernel(qkf_ref, qkb_ref, vf_ref, vb_ref, gf_ref, gb_ref, bias_ref,
                  hf_ref, hb_ref, c_sc, m_sc):
    L = ML_CHUNK

    @pl.when(pl.program_id(1) == 0)
    def _():
        c_sc[...] = jnp.zeros_like(c_sc)
        m_sc[...] = jnp.zeros_like(m_sc)

    r = lax.broadcasted_iota(jnp.int32, (L, L), 0)
    c = lax.broadcasted_iota(jnp.int32, (L, L), 1)
    lane = lax.broadcasted_iota(jnp.int32, (1, LANES), 1) - GATE_OFF
    is_forget = (lane >= 0) & (lane < N_GATES) & ((lane % (2 * ML_HEADS)) >= ML_HEADS)
    ones = jnp.ones((L, ML_DV), BF16)
    dirs = ((qkf_ref, vf_ref, gf_ref, hf_ref), (qkb_ref, vb_ref, gb_ref, hb_ref))

    gate_terms, early = [], {}
    for d, (qk_ref, v_ref, g_ref, h_ref) in enumerate(dirs):
        seen = (c <= r) if d == 0 else (c >= r)
        seen_b = seen.astype(F32).astype(BF16)
        gates = g_ref[0] + bias_ref[...]
        soft = jnp.minimum(gates, 0.0) - jnp.log1p(jnp.exp(-jnp.abs(gates)))
        logs = jnp.where(is_forget, soft, gates)
        logs_t = logs.T[GATE_OFF:GATE_OFF + N_GATES, :]
        cum_col = sum(jnp.dot(seen_b, p, preferred_element_type=F32) for p in _split3(logs))
        cum_row = sum(lax.dot_general(p, seen_b, (((1,), (1,)), ((), ())), preferred_element_type=F32)
                      for p in _split3(logs_t))
        gate_terms.append((seen, logs, logs_t, cum_col, cum_row))
        for h in range(ML_HEADS):
            sl = slice(h * LANES, (h + 1) * LANES)
            q = qk_ref[0, :, sl]
            k = qk_ref[0, :, ML_HEADS * ML_DK + h * LANES:ML_HEADS * ML_DK + (h + 1) * LANES]
            state = c_sc[d, h]
            qk = lax.dot_general(q, k, (((1,), (1,)), ((), ())), preferred_element_type=F32)
            inter = jnp.dot(q, state.astype(BF16), preferred_element_type=F32)
            early[d, h] = (k, state, qk, inter)

    for d, (qk_ref, v_ref, g_ref, h_ref) in enumerate(dirs):
        seen, logs, logs_t, cum_col, cum_row = gate_terms[d]
        last = L - 1 if d == 0 else 0
        for h in range(ML_HEADS):
            gi = d * 2 * ML_HEADS + h
            gf = gi + ML_HEADS
            b_col = cum_col[:, GATE_OFF + gf:GATE_OFF + gf + 1]
            b_row = cum_row[gf:gf + 1, :]
            li_col = logs[:, GATE_OFF + gi:GATE_OFF + gi + 1]
            li_row = logs_t[gi:gi + 1, :]
            b_end = b_row[:, last:last + 1]
            m_prev = m_sc[d, h][0:1, 0:1]
            sl = slice(h * LANES, (h + 1) * LANES)
            k, state, qk, inter = early[d, h]
            vext = jnp.concatenate([v_ref[0, :, sl], ones], axis=-1)

            b_all = jnp.broadcast_to(b_col, (L, L))
            logw = jnp.where(seen, b_all - b_row + li_row, -jnp.inf)
            m_t = jnp.broadcast_to(
                jnp.maximum(b_col + m_prev, jnp.max(logw, axis=-1, keepdims=True)), (L, L))
            w_state = jnp.exp(b_all + m_prev - m_t)
            s = qk * jnp.exp(logw - m_t)
            intra = jnp.dot(s.astype(BF16), vext, preferred_element_type=F32)
            num = w_state * inter[:, :ML_DV] + intra[:, :ML_DV]
            den = w_state * inter[:, ML_DV:] + intra[:, ML_DV:]
            h_ref[0, :, sl] = num / jnp.maximum(jnp.abs(den), jnp.exp(-m_t))

            g_all = b_end - b_all + jnp.broadcast_to(li_col, (L, L))
            m_new = jnp.maximum(b_end + m_prev, jnp.max(g_all, axis=0, keepdims=True)[:, 0:1])
            decay = jnp.exp(b_end + m_prev - m_new)
            wk = (jnp.exp(g_all - m_new) * k.astype(F32)).astype(BF16)
            upd = lax.dot_general(wk, vext, (((0,), (0,)), ((), ())), preferred_element_type=F32)
            c_sc[d, h] = decay * state + upd
            m_sc[d, h] = jnp.broadcast_to(m_new, (8, LANES))


def _mlstm(qk, v, side, gate_bias_row, t_len):
    bsz, s_len, _ = qk.shape
    L = ML_CHUNK
    n_chunks = s_len // L
    n_lat = t_len // L
    hw = ML_HEADS * ML_DV
    fwd = lambda b, j: (b, (j + n_lat) % n_chunks, 0)
    bwd = lambda b, j: (b, n_chunks - 1 - j, 0)
    return pl.pallas_call(
        _mlstm_kernel,
        grid=(bsz, n_chunks),
        in_specs=[pl.BlockSpec((1, L, 2 * hw), fwd), pl.BlockSpec((1, L, 2 * hw), bwd),
                  pl.BlockSpec((1, L, hw), fwd), pl.BlockSpec((1, L, hw), bwd),
                  pl.BlockSpec((1, L, LANES), fwd), pl.BlockSpec((1, L, LANES), bwd),
                  _const_spec((1, LANES))],
        out_specs=[pl.BlockSpec((1, L, hw), fwd), pl.BlockSpec((1, L, hw), bwd)],
        out_shape=[jax.ShapeDtypeStruct((bsz, s_len, hw), F32)] * 2,
        scratch_shapes=[pltpu.VMEM((2, ML_HEADS, ML_DK, 2 * ML_DV), F32),
                        pltpu.VMEM((2, ML_HEADS, 8, LANES), F32)],
        compiler_params=_params(("parallel", "arbitrary")),
        name="mlstm",
    )(qk, qk, v, v, side, side, gate_bias_row)


def _finish_kernel(*refs, n_stream, tm, t_len, with_readout, with_final, ff_chunk):
    h_refs = refs[:n_stream]
    it = iter(refs[n_stream:])
    if with_readout:
        alat_ref, actx_ref, hf_ref, hb_ref, o_ref, hn_ref = (next(it) for _ in range(6))
    else:
        mix_ref = next(it)
    mx_ref, mc_ref, wo_ref, n2_ref, w1_ref, w2_ref = (next(it) for _ in range(6))
    fn_ref = next(it) if with_final else None
    out_ref = next(it)

    is_ctx = _row_is_ctx(pl.program_id(1), tm, t_len)
    mod = lambda j: jnp.where(is_ctx, mc_ref[j:j + 1, :], mx_ref[0, j:j + 1, :])

    if with_readout:
        hsum = hf_ref[0] + hb_ref[0]
        parts = [_rms(hsum[:, h * ML_DV:(h + 1) * ML_DV], hn_ref[:, h * ML_DV:(h + 1) * ML_DV])
                 for h in range(ML_HEADS)]
        rec = jax.nn.sigmoid(o_ref[0].astype(F32)) * jnp.concatenate(parts, axis=-1)
        att = jnp.where(is_ctx, actx_ref[0], alat_ref[0])
        mix = jnp.concatenate([att, rec.astype(BF16)], axis=-1)
    else:
        mix = mix_ref[0]
    s = _stream_tile(h_refs, is_ctx) + mod(2) * jnp.dot(mix, wo_ref[...], preferred_element_type=F32)
    y = (_rms(s, n2_ref[...]) * (1.0 + mod(4)) + mod(3)).astype(BF16)
    d_ff = w1_ref.shape[1]
    acc = jnp.zeros(s.shape, F32)
    for c0 in range(0, d_ff, ff_chunk):
        u = jnp.maximum(jnp.dot(y, w1_ref[:, c0:c0 + ff_chunk], preferred_element_type=F32), 0.0)
        acc = acc + jnp.dot((u * u).astype(BF16), w2_ref[c0:c0 + ff_chunk, :],
                            preferred_element_type=F32)
    out = s + mod(5) * acc
    if with_final:
        out = _rms(out, fn_ref[...])
    out_ref[0] = out


def _finish(hs, mix_parts, modsx, modsc, w_out, norm2, w1, w2, *, t_len, n_rows, head_norm=None,
            final_norm=None):
    bsz, _, d = hs[0].shape
    tm = ROW_TILE
    ns = len(hs)
    with_readout = head_norm is not None
    with_final = final_norm is not None
    tok = lambda w: pl.BlockSpec((1, tm, w), lambda b, i: (b, i, 0))
    once = lambda shape: pl.BlockSpec(shape, lambda *_: (0,) * len(shape),
                                      pipeline_mode=pl.Buffered(1))
    args = list(hs) + list(mix_parts)
    specs = _stream_specs(hs, tm, t_len) + [tok(p.shape[-1]) for p in mix_parts]
    if with_readout:
        n_lat = t_len // tm
        aw = mix_parts[0].shape[-1]
        specs[ns] =pl.BlockSpec((1, tm, aw), lambda b, i: (b, jnp.minimum(i, n_lat - 1), 0))
        specs[ns + 1] =pl.BlockSpec((1, tm, aw), lambda b, i: (b, jnp.maximum(i - n_lat, 0), 0))
        args.append(head_norm.reshape(1, -1))
        specs.append(_const_spec((1, head_norm.shape[0])))
    args += [modsx, modsc, w_out, norm2.reshape(1, d), w1, w2]
    specs += [pl.BlockSpec((1, 8, d), lambda b, i: (b, 0, 0)), _const_spec((8, d)),
              once(w_out.shape), _const_spec((1, d)), once(w1.shape), once(w2.shape)]
    if with_final:
        args.append(final_norm.reshape(1, d))
        specs.append(_const_spec((1, d)))
    kern = functools.partial(_finish_kernel, n_stream=ns, tm=tm, t_len=t_len, with_readout=with_readout,
                             with_final=with_final, ff_chunk=1024)
    return pl.pallas_call(
        kern,
        grid=(bsz, n_rows // tm),
        in_specs=specs,
        out_specs=tok(d),
        out_shape=jax.ShapeDtypeStruct((bsz, n_rows, d), F32),
        compiler_params=_params(("parallel", "parallel")),
        name="finish",
    )(*args)


def _rope_tables(t_len, ctx_len, d_rot, lane0):
    n_freq = d_rot // 4
    n = 2 * n_freq
    rel = np.arange(LANES) - lane0
    first = (rel >= 0) & (rel < n)
    second = (rel >= n) & (rel < 2 * n)
    idx = np.clip(np.where(second, rel - n, rel), 0, n - 1)
    freqs = ROPE_THETA ** (-jnp.arange(n_freq, dtype=F32) / n_freq)
    t = jnp.arange(t_len + ctx_len)[:, None]
    pos = jnp.where(jnp.asarray(idx // n_freq == 0)[None, :], t // GRID_W, t % GRID_W).astype(F32)
    ang = pos * freqs[idx % n_freq][None, :]
    live = t < t_len
    cos, sin = jnp.cos(ang), jnp.sin(ang)
    c = jnp.where(live & jnp.asarray(first | second)[None, :], cos, 1.0)
    s1 = jnp.where(live & jnp.asarray(second)[None, :], sin, 0.0)
    s2 = jnp.where(live & jnp.asarray(first)[None, :], -sin, 0.0)
    return c, s1, s2


def _mla_rope_cols():
    nf = MLA_ROPE // 4
    first = [a * 2 * nf + f for a in range(2) for f in range(nf)]
    second = [a * 2 * nf + nf + f for a in range(2) for f in range(nf)]
    return np.array(first + second)


def _gqa_head_cols():
    nf = GQA_DH // 4
    first = [a * 2 * nf + f for a in range(2) for f in range(nf)]
    second = [a * 2 * nf + nf + f for a in range(2) for f in range(nf)]
    return np.array(first + second)


def kernel(x, c, ctx, c_ctx,
           l0_ada_w, l0_ada_b, l0_norm1, l0_w_in, l0_mla_q_norm, l0_mla_w_uq, l0_mla_kv_norm, l0_mla_w_ukv,
           l0_ml_conv, l0_ml_gate_b, l0_ml_head_norm, l0_w_out, l0_norm2, l0_w1, l0_w2,
           l1_ada_w, l1_ada_b, l1_norm1, l1_w_in, l1_q_norm, l1_k_norm, l1_w_out, l1_norm2, l1_w1, l1_w2,
           final_norm):
    bsz, t_len, d = x.shape
    ctx_len = ctx.shape[1]
    s_len = t_len + ctx_len

    cv = jnp.zeros((8, d), F32).at[:bsz].set(c).at[bsz].set(c_ctx)

    def mods(ada_w, ada_b):
        m = _adaln(cv, ada_w, ada_b).reshape(8, N_MOD, d)
        pad = jnp.zeros((8 - N_MOD, d), F32)
        modsx = jnp.concatenate([m[:bsz], jnp.broadcast_to(pad, (bsz,) + pad.shape)], axis=1)
        modsc = jnp.concatenate([m[bsz], pad], axis=0)
        return modsx, modsc

    modsx, modsc = mods(l0_ada_w, l0_ada_b)
    o = np.cumsum([0, MLA_Q_RANK, MLA_KV_RANK, MLA_ROPE, 2 * ML_HEADS * ML_DK, ML_HEADS * ML_DV,
                   ML_HEADS * ML_DV, N_GATES])
    w_cq, w_ckv, w_kpe, w_qk, w_v, w_o, w_g = (l0_w_in[:, o[i]:o[i + 1]] for i in range(7))
    side_pad = jnp.zeros((d, LANES - MLA_ROPE - N_GATES), F32)
    w_in0 = jnp.concatenate([w_qk, w_v, w_o, w_cq, w_ckv, w_kpe[:, _mla_rope_cols()], w_g, side_pad],
                            axis=1).astype(BF16)
    widths0 = (2 * ML_HEADS * ML_DK, ML_HEADS * ML_DV, ML_HEADS * ML_DV, MLA_Q_RANK + MLA_KV_RANK, LANES)
    qk_pre, v_ml, o_pre, lat, side = _inproj((x, ctx), l0_norm1, modsx, modsc, w_in0, widths0,
                                             (BF16, BF16, BF16, BF16, F32), t_len, s_len)

    dq = MLA_NOPE + MLA_ROPE
    head_cols = np.concatenate([np.arange(MLA_NOPE), MLA_NOPE + _mla_rope_cols()])
    wq = l0_mla_w_uq.reshape(MLA_Q_RANK, MLA_HEADS, dq)[:, :, head_cols]
    wq = jnp.pad(wq, ((0, 0), (0, 0), (0, LANES - dq))).reshape(MLA_Q_RANK, MLA_HEADS * LANES).astype(BF16)
    wkv = l0_mla_w_ukv.reshape(MLA_KV_RANK, MLA_HEADS, MLA_NOPE + MLA_V)
    wk_nope = jnp.pad(wkv[:, :, :MLA_NOPE], ((0, 0), (0, 0), (0, LANES - MLA_NOPE)))
    place = np.zeros((LANES, MLA_HEADS, LANES), np.float32)
    for hh in range(MLA_HEADS):
        place[KPE_OFF + np.arange(MLA_ROPE), hh, MLA_NOPE + np.arange(MLA_ROPE)] = 1.0
    wk = jnp.concatenate([wk_nope, jnp.asarray(place)], axis=0).reshape(
        MLA_KV_RANK + LANES, MLA_HEADS * LANES).astype(BF16)
    wv = wkv[:, :, MLA_NOPE:].reshape(MLA_KV_RANK, MLA_HEADS * MLA_V).astype(BF16)
    tabs_mla = _rope_tables(t_len, ctx_len, MLA_ROPE, MLA_NOPE)
    q_mla, k_mla, v_mla = _mla_prep(lat, side, tabs_mla, l0_mla_q_norm, l0_mla_kv_norm, wq, wk, wv)
    mla_att = functools.partial(_attention, q_mla, k_mla, v_mla, n_groups=MLA_HEADS // 2, n_heads=2,
                                k_shared=False, v_pair=True, t_len=t_len)
    a_lat, a_ctx = mla_att(latent_queries=True), mla_att(latent_queries=False)

    qk = _conv_silu(qk_pre, l0_ml_conv, t_len)
    bias_row = jnp.zeros((1, LANES), F32).at[0, GATE_OFF:GATE_OFF + N_GATES].set(l0_ml_gate_b)
    hf, hb = _mlstm(qk, v_ml, side, bias_row, t_len)
    h = _finish((x, ctx), (a_lat, a_ctx, hf, hb, o_pre), modsx, modsc, l0_w_out.astype(BF16),
                l0_norm2, l0_w1.astype(BF16), l0_w2.astype(BF16), t_len=t_len, n_rows=s_len,
                head_norm=l0_ml_head_norm)

    modsx, modsc = mods(l1_ada_w, l1_ada_b)
    hc = _gqa_head_cols()
    n_qk = GQA_HEADS + GQA_KV_HEADS
    qk_cols = (np.arange(n_qk)[:, None] * GQA_DH + hc[None, :]).reshape(-1)
    n_q = GQA_HEADS * GQA_DH
    wqv_t = jnp.concatenate([l1_w_in[:, qk_cols[:n_q]], l1_w_in[:, n_qk * GQA_DH:]], axis=1).T.astype(BF16)
    wk1 = l1_w_in[:, qk_cols[n_q:]].astype(BF16)
    tabs_gqa = _rope_tables(t_len, ctx_len, GQA_DH, 0)
    q_gqa, k_gqa, v_gqa = _gqa_inproj(h, l1_norm1, modsx, modsc, wqv_t, wk1, tabs_gqa,
                                      l1_q_norm[hc], l1_k_norm[hc], t_len)
    att = _attention(q_gqa, k_gqa, v_gqa, n_groups=GQA_KV_HEADS, n_heads=GQA_GROUP, k_shared=True,
                     v_pair=False, t_len=t_len, latent_queries=True)
    return _finish((h,), (att,), modsx, modsc, l1_w_out.astype(BF16), l1_norm2,
                   l1_w1.astype(BF16), l1_w2.astype(BF16), t_len=t_len, n_rows=t_len,
                   final_norm=final_norm)
```

```python
import functools

import numpy as np
import jax
import jax.numpy as jnp
from jax import lax
from jax.experimental import pallas as pl
from jax.experimental.pallas import tpu as pltpu

F32 = jnp.float32
BF16 = jnp.bfloat16

GRID_W = 64
NORM_EPS = 1e-6
ROPE_THETA = 10000.0
N_MOD = 6
MLA_HEADS = 8
MLA_Q_RANK = 256
MLA_KV_RANK = 128
MLA_NOPE = 64
MLA_ROPE = 32
MLA_V = 64
ML_HEADS = 4
ML_DK = 128
ML_DV = 128
ML_CHUNK = 128
N_GATES = 4 * ML_HEADS
GQA_HEADS = 8
GQA_KV_HEADS = 2
GQA_GROUP = GQA_HEADS // GQA_KV_HEADS
GQA_DH = 128

LANES = 128
V7X_VMEM_BYTES = 64 * 1024 * 1024
VMEM_LIMIT = V7X_VMEM_BYTES * 7 // 8

ROW_TILE = 256
ATT_TQ = 1024
ATT_TK = 256
ATT_UNROLL = 8
KPE_OFF = 0
GATE_OFF = 32


def _params(sem):
    return pltpu.CompilerParams(dimension_semantics=sem, vmem_limit_bytes=VMEM_LIMIT)


def _rms(x, g):
    var = jnp.mean(x * x, axis=-1, keepdims=True)
    return x * lax.rsqrt(var + NORM_EPS) * g


def _const_spec(shape):
    nd = len(shape)
    return pl.BlockSpec(shape, lambda *_: (0,) * nd)


def _adaln_kernel(c_ref, w_ref, b_ref, o_ref):
    c = c_ref[...]
    a = c * jax.nn.sigmoid(c)
    o_ref[...] = jnp.dot(a.astype(BF16), w_ref[...].astype(BF16),
                         preferred_element_type=F32) + b_ref[...]


def _adaln(cv, w, b):
    d = cv.shape[1]
    n = w.shape[1]
    tn = n // 4
    return pl.pallas_call(
        _adaln_kernel,
        grid=(n // tn,),
        in_specs=[pl.BlockSpec((8, d), lambda j: (0, 0)),
                  pl.BlockSpec((d, tn), lambda j: (0, j)),
                  pl.BlockSpec((1, tn), lambda j: (0, j))],
        out_specs=pl.BlockSpec((8, tn), lambda j: (0, j)),
        out_shape=jax.ShapeDtypeStruct((8, n), F32),
        compiler_params=_params(("arbitrary",)),
        name="adaln",
    )(cv, w, b.reshape(1, n))


def _row_is_ctx(tile, tm, t_len):
    row = tile * tm + lax.broadcasted_iota(jnp.int32, (tm, 1), 0)
    return row >= t_len


def _stream_specs(hs, tm, t_len):
    d = hs[0].shape[-1]
    if len(hs) == 1:
        return [pl.BlockSpec((1, tm, d), lambda b, i: (b, i, 0))]
    n_lat = t_len // tm
    return [pl.BlockSpec((1, tm, d), lambda b, i: (b, jnp.minimum(i, n_lat - 1), 0)),
            pl.BlockSpec((1, tm, d), lambda b, i: (b, jnp.maximum(i - n_lat, 0), 0))]


def _stream_tile(h_refs, is_ctx):
    if len(h_refs) == 1:
        return h_refs[0][0]
    return jnp.where(is_ctx, h_refs[1][0], h_refs[0][0])


def _inproj_kernel(*refs, n_stream, tm, t_len, widths):
    h_refs, (g_ref, mx_ref, mc_ref, w_ref), out_refs = (
        refs[:n_stream], refs[n_stream:n_stream + 4], refs[n_stream + 4:])
    is_ctx = _row_is_ctx(pl.program_id(1), tm, t_len)
    shift = jnp.where(is_ctx, mc_ref[0:1, :], mx_ref[0, 0:1, :])
    scale = jnp.where(is_ctx, mc_ref[1:2, :], mx_ref[0, 1:2, :])
    y = (_rms(_stream_tile(h_refs, is_ctx), g_ref[...]) * (1.0 + scale) + shift).astype(BF16)
    off = 0
    for o_ref, width in zip(out_refs, widths):
        o_ref[0] = jnp.dot(y, w_ref[:, off:off + width],
                           preferred_element_type=F32).astype(o_ref.dtype)
        off += width


def _inproj(hs, g, modsx, modsc, w, widths, dtypes, t_len, s_len):
    bsz, _, d = hs[0].shape
    tm = ROW_TILE
    kern = functools.partial(_inproj_kernel, n_stream=len(hs), tm=tm, t_len=t_len, widths=widths)
    return pl.pallas_call(
        kern,
        grid=(bsz, s_len // tm),
        in_specs=_stream_specs(hs, tm, t_len) + [
            _const_spec((1, d)),
            pl.BlockSpec((1, 8, d), lambda b, i: (b, 0, 0)),
            _const_spec((8, d)),
            _const_spec(w.shape)],
        out_specs=[pl.BlockSpec((1, tm, wd), lambda b, i: (b, i, 0)) for wd in widths],
        out_shape=[jax.ShapeDtypeStruct((bsz, s_len, wd), dt) for wd, dt in zip(widths, dtypes)],
        compiler_params=_params(("parallel", "parallel")),
        name="inproj",
    )(*hs, g.reshape(1, d), modsx, modsc, w)


def _rope(x, c, s1, s2, half):
    return x * c + pltpu.roll(x, half, 1) * s1 + pltpu.roll(x, LANES - half, 1) * s2


LOG2E = 1.4426950408889634
BF16_SUBLANES = 16
VT_ROWS = LANES + BF16_SUBLANES


def _store_vt(vt_ref, vt, tm):
    for g in range(vt.shape[0] // LANES):
        vt_ref[0, g * VT_ROWS:g * VT_ROWS + LANES, :] = vt[g * LANES:(g + 1) * LANES].astype(BF16)
        vt_ref[0, g * VT_ROWS + LANES:(g + 1) * VT_ROWS, :] = jnp.ones((BF16_SUBLANES, tm), BF16)


def _mla_prep_kernel(lat_ref, side_ref, c_ref, s1_ref, s2_ref, ct_ref, st_ref, qn_ref, kvn_ref,
                     wq_ref, wk_ref, wv_ref, qt_ref, k_ref, vt_ref, *, scale):
    lat = lat_ref[0].astype(F32)
    tm = lat.shape[0]
    cqn = _rms(lat[:, :MLA_Q_RANK], qn_ref[...]).astype(BF16)
    ckvn = _rms(lat[:, MLA_Q_RANK:], kvn_ref[...]).astype(BF16)
    nt = (((1,), (1,)), ((), ()))
    half = MLA_ROPE // 2
    qt = lax.dot_general(wq_ref[...], cqn, nt, preferred_element_type=F32) * scale
    ct, st = ct_ref[...], st_ref[...]
    for h in range(MLA_HEADS):
        r0 = h * LANES + MLA_NOPE
        a, b = qt[r0:r0 + half], qt[r0 + half:r0 + 2 * half]
        qt_ref[0, h * LANES:r0, :] = qt[h * LANES:r0].astype(BF16)
        qt_ref[0, r0:r0 + half, :] = (a * ct - b * st).astype(BF16)
        qt_ref[0, r0 + half:r0 + 2 * half, :] = (b * ct + a * st).astype(BF16)
        qt_ref[0, r0 + 2 * half:(h + 1) * LANES, :] = jnp.zeros((LANES - MLA_NOPE - 2 * half, tm), BF16)
    _store_vt(vt_ref, lax.dot_general(wv_ref[...], ckvn, nt, preferred_element_type=F32), tm)
    c, s1, s2 = c_ref[...], s1_ref[...], s2_ref[...]
    kin = jnp.concatenate([ckvn, side_ref[0].astype(BF16)], axis=-1)
    k = jnp.dot(kin, wk_ref[...], preferred_element_type=F32)
    for h in range(MLA_HEADS):
        sl = slice(h * LANES, (h + 1) * LANES)
        k_ref[0, :, sl] = _rope(k[:, sl], c, s1, s2, half).astype(BF16)


def _mla_prep(lat, side, tabs, qn, kvn, wq, wk, wv):
    bsz, s_len, lw = lat.shape
    tm = ROW_TILE
    hw = MLA_HEADS * LANES
    vw = MLA_HEADS * MLA_V
    tok = lambda w: pl.BlockSpec((1, tm, w), lambda b, i: (b, i, 0))
    tok_t = lambda w: pl.BlockSpec((1, w, tm), lambda b, i: (b, 0, i))
    tab = pl.BlockSpec((tm, LANES), lambda b, i: (i, 0))
    half = MLA_ROPE // 2
    tab_t = pl.BlockSpec((half, tm), lambda b, i: (0, i))
    c, s1, s2 = tabs
    tabs = (c, s1, s2, c[:, MLA_NOPE:MLA_NOPE + half].T, s1[:, MLA_NOPE + half:MLA_NOPE + 2 * half].T)
    kern = functools.partial(_mla_prep_kernel, scale=LOG2E * (MLA_NOPE + MLA_ROPE) ** -0.5)
    return pl.pallas_call(
        kern,
        grid=(bsz, s_len // tm),
        in_specs=[tok(lw), tok(LANES), tab, tab, tab, tab_t, tab_t,
                  _const_spec((1, MLA_Q_RANK)), _const_spec((1, MLA_KV_RANK)),
                  _const_spec(wq.shape), _const_spec(wk.shape), _const_spec(wv.shape)],
        out_specs=[tok_t(hw), tok(hw), tok_t(vw // LANES * VT_ROWS)],
        out_shape=[jax.ShapeDtypeStruct((bsz, hw, s_len), BF16),
                   jax.ShapeDtypeStruct((bsz, s_len, hw), BF16),
                   jax.ShapeDtypeStruct((bsz, vw // LANES * VT_ROWS, s_len), BF16)],
        compiler_params=_params(("parallel", "parallel")),
        name="mla_prep",
    )(lat, side, *tabs, qn.reshape(1, -1), kvn.reshape(1, -1), wq, wk, wv)


def _gqa_inproj_kernel(h_ref, g_ref, mx_ref, mc_ref, wqv_ref, wk_ref, ct_ref, st_ref,
                       c_ref, s1_ref, s2_ref, qnb_ref, kn_ref, qt_ref, k_ref, vt_ref,
                       *, tm, t_len, scale):
    is_ctx = _row_is_ctx(pl.program_id(1), tm, t_len)
    shift = jnp.where(is_ctx, mc_ref[0:1, :], mx_ref[0, 0:1, :])
    gain = jnp.where(is_ctx, mc_ref[1:2, :], mx_ref[0, 1:2, :])
    y = (_rms(h_ref[0], g_ref[...]) * (1.0 + gain) + shift).astype(BF16)
    qv = lax.dot_general(wqv_ref[...], y, (((1,), (1,)), ((), ())), preferred_element_type=F32)
    half = GQA_DH // 2
    ct, st = ct_ref[...], st_ref[...]
    qnb = qnb_ref[...] * scale
    for h in range(GQA_HEADS):
        x = qv[h * GQA_DH:(h + 1) * GQA_DH]
        xn = x * lax.rsqrt(jnp.mean(x * x, axis=0, keepdims=True) + NORM_EPS) * qnb
        a, b = xn[:half], xn[half:]
        qt_ref[0, h * GQA_DH:h * GQA_DH + half, :] = (a * ct - b * st).astype(BF16)
        qt_ref[0, h * GQA_DH + half:(h + 1) * GQA_DH, :] = (b * ct + a * st).astype(BF16)
    _store_vt(vt_ref, qv[GQA_HEADS * GQA_DH:], tm)
    k = jnp.dot(y, wk_ref[...], preferred_element_type=F32)
    c, s1, s2 = c_ref[...], s1_ref[...], s2_ref[...]
    for h in range(GQA_KV_HEADS):
        sl = slice(h * LANES, (h + 1) * LANES)
        k_ref[0, :, sl] = _rope(_rms(k[:, sl], kn_ref[...]), c, s1, s2, half).astype(BF16)


def _gqa_inproj(h, g, modsx, modsc, wqv_t, wk, tabs, qn, kn, t_len):
    bsz, s_len, d = h.shape
    tm = ROW_TILE
    qw = GQA_HEADS * GQA_DH
    kw = GQA_KV_HEADS * GQA_DH
    half = GQA_DH // 2
    c, s1, s2 = tabs
    ct, st = c[:, :half].T, s1[:, half:].T
    qnb = jnp.broadcast_to(qn[:, None], (GQA_DH, tm))
    tab = pl.BlockSpec((tm, LANES), lambda b, i: (i, 0))
    tab_t = pl.BlockSpec((half, tm), lambda b, i: (0, i))
    kern = functools.partial(_gqa_inproj_kernel, tm=tm, t_len=t_len, scale=LOG2E * GQA_DH ** -0.5)
    return pl.pallas_call(
        kern,
        grid=(bsz, s_len // tm),
        in_specs=[pl.BlockSpec((1, tm, d), lambda b, i: (b, i, 0)),
                  _const_spec((1, d)),
                  pl.BlockSpec((1, 8, d), lambda b, i: (b, 0, 0)),
                  _const_spec((8, d)),
                  _const_spec(wqv_t.shape), _const_spec(wk.shape),
                  tab_t, tab_t, tab, tab, tab,
                  _const_spec((GQA_DH, tm)), _const_spec((1, GQA_DH))],
        out_specs=[pl.BlockSpec((1, qw, tm), lambda b, i: (b, 0, i)),
                   pl.BlockSpec((1, tm, kw), lambda b, i: (b, i, 0)),
                   pl.BlockSpec((1, kw // LANES * VT_ROWS, tm), lambda b, i: (b, 0, i))],
        out_shape=[jax.ShapeDtypeStruct((bsz, qw, s_len), BF16),
                   jax.ShapeDtypeStruct((bsz, s_len, kw), BF16),
                   jax.ShapeDtypeStruct((bsz, kw // LANES * VT_ROWS, s_len), BF16)],
        compiler_params=_params(("parallel", "parallel")),
        name="gqa_inproj",
    )(h, g.reshape(1, d), modsx, modsc, wqv_t, wk, ct, st, c, s1, s2, qnb, kn.reshape(1, -1))


def _attn_kernel(qt_ref, k_ref, vt_ref, o_ref, m_sc, l_sc, acc_sc, *,
                 n_heads, k_shared, v_pair, latent_queries, tq, tk, t_len, s_len):
    def scores(g, kc):
        kg = kc if k_shared else kc[:, g * LANES:(g + 1) * LANES]
        return jnp.dot(kg, qt_ref[0, g * LANES:(g + 1) * LANES, :], preferred_element_type=F32)

    def k_chunk(j):
        return k_ref[0, pl.ds(pl.multiple_of(j * tk, tk), tk), :]

    def update(g, st, vte, first):
        m_new = jnp.max(st, axis=0, keepdims=True)
        if not first:
            m_old = m_sc[g][0:1]
            m_new = jnp.maximum(m_old, m_new)
            alpha = jnp.exp2(m_old - m_new)
        p = jnp.exp2(st - m_new).astype(BF16)
        res = jnp.dot(vte, p, preferred_element_type=F32)
        if first:
            acc_sc[g] = res[:LANES]
            l_sc[g] = res[LANES:LANES + 8]
        else:
            acc_sc[g] = alpha * acc_sc[g] + res[:LANES]
            l_sc[g] = alpha * l_sc[g] + res[LANES:LANES + 8]
        m_sc[g] = jnp.broadcast_to(m_new, (8, tq))

    kc = k_ref[0, t_len:s_len, :]
    vte = vt_ref[0, :, t_len:s_len]
    st_next = scores(0, kc)
    for g in range(n_heads):
        st = st_next
        if g + 1 < n_heads:
            st_next = scores(g + 1, kc)
        elif latent_queries:
            st_next = scores(0, k_chunk(0))
        update(g, st, vte, True)

    if latent_queries:
        n_chunks = t_len // tk

        def body(j, st_next):
            kc = k_chunk(j)
            vte = vt_ref[0, :, pl.ds(pl.multiple_of(j * tk, tk), tk)]
            for g in range(n_heads):
                st = st_next
                if g + 1 < n_heads:
                    st_next = scores(g + 1, kc)
                else:
                    st_next = scores(0, k_chunk(jnp.minimum(j + 1, n_chunks - 1)))
                update(g, st, vte, False)
            return st_next

        lax.fori_loop(0, n_chunks, body, st_next, unroll=ATT_UNROLL)

    outs = [acc_sc[g] / l_sc[g][0:1] for g in range(n_heads)]
    if v_pair:
        row = lax.broadcasted_iota(jnp.int32, (LANES, tq), 0)
        o_ref[0] = jnp.where(row < LANES // 2, outs[0], outs[1]).T.astype(o_ref.dtype)
    else:
        for g in range(n_heads):
            o_ref[0, :, g * LANES:(g + 1) * LANES] = outs[g].T.astype(o_ref.dtype)


def _attention(qt, k, vt, *, n_groups, n_heads, k_shared, v_pair, t_len, latent_queries):
    bsz, s_len, _ = k.shape
    tq = ATT_TQ if latent_queries else s_len - t_len
    n_q_rows = t_len if latent_queries else s_len - t_len
    q0 = 0 if latent_queries else t_len // tq
    tk = ATT_TK
    qw = n_heads * LANES
    kw = LANES if k_shared else qw
    ow = LANES if v_pair else qw
    kern = functools.partial(_attn_kernel, n_heads=n_heads, k_shared=k_shared, v_pair=v_pair,
                             latent_queries=latent_queries, tq=tq, tk=tk, t_len=t_len, s_len=s_len)
    return pl.pallas_call(
        kern,
        grid=(bsz, n_groups, n_q_rows // tq),
        in_specs=[pl.BlockSpec((1, qw, tq), lambda b, g, i: (b, g, q0 + i)),
                  pl.BlockSpec((1, s_len, kw), lambda b, g, i: (b, 0, g)),
                  pl.BlockSpec((1, VT_ROWS, s_len), lambda b, g, i: (b, g, 0))],
        out_specs=pl.BlockSpec((1, tq, ow), lambda b, g, i: (b, i, g)),
        out_shape=jax.ShapeDtypeStruct((bsz, n_q_rows, n_groups * ow), BF16),
        scratch_shapes=[pltpu.VMEM((n_heads, 8, tq), F32),
                        pltpu.VMEM((n_heads, 8, tq), F32),
                        pltpu.VMEM((n_heads, LANES, tq), F32)],
        compiler_params=_params(("parallel", "parallel", "arbitrary")),
        name="attention",
    )(qt, k, vt)


def _conv_kernel(cur_ref, prev_ref, next_ref, w_ref, o_ref, *, tm, t_len, s_len, qscale):
    i = pl.program_id(1)
    x = cur_ref[0].astype(F32)
    row = i * tm + lax.broadcasted_iota(jnp.int32, (tm, 1), 0)
    local = lax.broadcasted_iota(jnp.int32, (tm, 1), 0)
    prev_row = prev_ref[0, 15:16, :].astype(F32)
    next_row = next_ref[0, 0:1, :].astype(F32)
    xm1 = jnp.where(local == 0, prev_row, pltpu.roll(x, 1, 0))
    xp1 = jnp.where(local == tm - 1, next_row, pltpu.roll(x, tm - 1, 0))
    xm1 = jnp.where((row == 0) | (row == t_len), 0.0, xm1)
    xp1 = jnp.where((row == t_len - 1) | (row == s_len - 1), 0.0, xp1)
    y = w_ref[0:1, :] * xm1 + w_ref[1:2, :] * x + w_ref[2:3, :] * xp1
    y = y * jax.nn.sigmoid(y)
    half = y.shape[1] // 2
    o_ref[0, :, :half] = (y[:, :half] * qscale).astype(BF16)
    o_ref[0, :, half:] = y[:, half:].astype(BF16)


def _conv_silu(qk_pre, conv_w, t_len):
    bsz, s_len, w = qk_pre.shape
    tm = ROW_TILE
    r16 = tm // 16
    n16 = s_len // 16
    kern = functools.partial(_conv_kernel, tm=tm, t_len=t_len, s_len=s_len, qscale=ML_DK ** -0.5)
    return pl.pallas_call(
        kern,
        grid=(bsz, s_len // tm),
        in_specs=[pl.BlockSpec((1, tm, w), lambda b, i: (b, i, 0)),
                  pl.BlockSpec((1, 16, w), lambda b, i: (b, jnp.maximum(i * r16 - 1, 0), 0)),
                  pl.BlockSpec((1, 16, w), lambda b, i: (b, jnp.minimum((i + 1) * r16, n16 - 1), 0)),
                  _const_spec((8, w))],
        out_specs=pl.BlockSpec((1, tm, w), lambda b, i: (b, i, 0)),
        out_shape=jax.ShapeDtypeStruct((bsz, s_len, w), BF16),
        compiler_params=_params(("parallel", "parallel")),
        name="conv_silu",
    )(qk_pre, qk_pre, qk_pre, jnp.pad(conv_w, ((0, 8 - conv_w.shape[0]), (0, 0))))


def _split3(a):
    a1 = a.astype(BF16)
    r1 = a - a1.astype(F32)
    a2 = r1.astype(BF16)
    a3 = (r1 - a2.astype(F32)).astype(BF16)
    return a1, a2, a3


def _mlstm_kernel(qkf_ref, qkb_ref, vf_ref, vb_ref, gf_ref, gb_ref, bias_ref,
                  hf_ref, hb_ref, c_sc, m_sc):
    L = ML_CHUNK

    @pl.when(pl.program_id(1) == 0)
    def _():
        c_sc[...] = jnp.zeros_like(c_sc)
        m_sc[...] = jnp.zeros_like(m_sc)

    r = lax.broadcasted_iota(jnp.int32, (L, L), 0)
    c = lax.broadcasted_iota(jnp.int32, (L, L), 1)
    lane = lax.broadcasted_iota(jnp.int32, (1, LANES), 1) - GATE_OFF
    is_forget = (lane >= 0) & (lane < N_GATES) & ((lane % (2 * ML_HEADS)) >= ML_HEADS)
    ones = jnp.ones((L, ML_DV), BF16)
    dirs = ((qkf_ref, vf_ref, gf_ref, hf_ref), (qkb_ref, vb_ref, gb_ref, hb_ref))

    gate_terms, early = [], {}
    for d, (qk_ref, v_ref, g_ref, h_ref) in enumerate(dirs):
        seen = (c <= r) if d == 0 else (c >= r)
        seen_b = seen.astype(F32).astype(BF16)
        gates = g_ref[0] + bias_ref[...]
        soft = jnp.minimum(gates, 0.0) - jnp.log1p(jnp.exp(-jnp.abs(gates)))
        logs = jnp.where(is_forget, soft, gates)
        logs_t = logs.T[GATE_OFF:GATE_OFF + N_GATES, :]
        cum_col = sum(jnp.dot(seen_b, p, preferred_element_type=F32) for p in _split3(logs))
        cum_row = sum(lax.dot_general(p, seen_b, (((1,), (1,)), ((), ())), preferred_element_type=F32)
                      for p in _split3(logs_t))
        gate_terms.append((seen, logs, logs_t, cum_col, cum_row))
        for h in range(ML_HEADS):
            sl = slice(h * LANES, (h + 1) * LANES)
            q = qk_ref[0, :, sl]
            k = qk_ref[0, :, ML_HEADS * ML_DK + h * LANES:ML_HEADS * ML_DK + (h + 1) * LANES]
            state = c_sc[d, h]
            qk = lax.dot_general(q, k, (((1,), (1,)), ((), ())), preferred_element_type=F32)
            inter = jnp.dot(q, state.astype(BF16), preferred_element_type=F32)
            early[d, h] = (k, state, qk, inter)

    for d, (qk_ref, v_ref, g_ref, h_ref) in enumerate(dirs):
        seen, logs, logs_t, cum_col, cum_row = gate_terms[d]
        last = L - 1 if d == 0 else 0
        for h in range(ML_HEADS):
            gi = d * 2 * ML_HEADS + h
            gf = gi + ML_HEADS
            b_col = cum_col[:, GATE_OFF + gf:GATE_OFF + gf + 1]
            b_row = cum_row[gf:gf + 1, :]
            li_col = logs[:, GATE_OFF + gi:GATE_OFF + gi + 1]
            li_row = logs_t[gi:gi + 1, :]
            b_end = b_row[:, last:last + 1]
            m_prev = m_sc[d, h][0:1, 0:1]
            sl = slice(h * LANES, (h + 1) * LANES)
            k, state, qk, inter = early[d, h]
            vext = jnp.concatenate([v_ref[0, :, sl], ones], axis=-1)

            b_all = jnp.broadcast_to(b_col, (L, L))
            logw = jnp.where(seen, b_all - b_row + li_row, -jnp.inf)
            m_t = jnp.broadcast_to(
                jnp.maximum(b_col + m_prev, jnp.max(logw, axis=-1, keepdims=True)), (L, L))
            w_state = jnp.exp(b_all + m_prev - m_t)
            s = qk * jnp.exp(logw - m_t)
            intra = jnp.dot(s.astype(BF16), vext, preferred_element_type=F32)
            num = w_state * inter[:, :ML_DV] + intra[:, :ML_DV]
            den = w_state * inter[:, ML_DV:] + intra[:, ML_DV:]
            h_ref[0, :, sl] = num / jnp.maximum(jnp.abs(den), jnp.exp(-m_t))

            g_all = b_end - b_all + jnp.broadcast_to(li_col, (L, L))
            m_new = jnp.maximum(b_end + m_prev, jnp.max(g_all, axis=0, keepdims=True)[:, 0:1])
            decay = jnp.exp(b_end + m_prev - m_new)
            wk = (jnp.exp(g_all - m_new) * k.astype(F32)).astype(BF16)
            upd = lax.dot_general(wk, vext, (((0,), (0,)), ((), ())), preferred_element_type=F32)
            c_sc[d, h] = decay * state + upd
            m_sc[d, h] = jnp.broadcast_to(m_new, (8, LANES))


def _mlstm(qk, v, side, gate_bias_row, t_len):
    bsz, s_len, _ = qk.shape
    L = ML_CHUNK
    n_chunks = s_len // L
    n_lat = t_len // L
    hw = ML_HEADS * ML_DV
    fwd = lambda b, j: (b, (j + n_lat) % n_chunks, 0)
    bwd = lambda b, j: (b, n_chunks - 1 - j, 0)
    return pl.pallas_call(
        _mlstm_kernel,
        grid=(bsz, n_chunks),
        in_specs=[pl.BlockSpec((1, L, 2 * hw), fwd), pl.BlockSpec((1, L, 2 * hw), bwd),
                  pl.BlockSpec((1, L, hw), fwd), pl.BlockSpec((1, L, hw), bwd),
                  pl.BlockSpec((1, L, LANES), fwd), pl.BlockSpec((1, L, LANES), bwd),
                  _const_spec((1, LANES))],
        out_specs=[pl.BlockSpec((1, L, hw), fwd), pl.BlockSpec((1, L, hw), bwd)],
        out_shape=[jax.ShapeDtypeStruct((bsz, s_len, hw), F32)] * 2,
        scratch_shapes=[pltpu.VMEM((2, ML_HEADS, ML_DK, 2 * ML_DV), F32),
                        pltpu.VMEM((2, ML_HEADS, 8, LANES), F32)],
        compiler_params=_params(("parallel", "arbitrary")),
        name="mlstm",
    )(qk, qk, v, v, side, side, gate_bias_row)


def _finish_kernel(*refs, n_stream, tm, t_len, with_readout, with_final, ff_chunk):
    h_refs = refs[:n_stream]
    it = iter(refs[n_stream:])
    if with_readout:
        alat_ref, actx_ref, hf_ref, hb_ref, o_ref, hn_ref = (next(it) for _ in range(6))
    else:
        mix_ref = next(it)
    mx_ref, mc_ref, wo_ref, n2_ref, w1_ref, w2_ref = (next(it) for _ in range(6))
    fn_ref = next(it) if with_final else None
    out_ref = next(it)

    is_ctx = _row_is_ctx(pl.program_id(1), tm, t_len)
    mod = lambda j: jnp.where(is_ctx, mc_ref[j:j + 1, :], mx_ref[0, j:j + 1, :])

    if with_readout:
        hsum = hf_ref[0] + hb_ref[0]
        parts = [_rms(hsum[:, h * ML_DV:(h + 1) * ML_DV], hn_ref[:, h * ML_DV:(h + 1) * ML_DV])
                 for h in range(ML_HEADS)]
        rec = jax.nn.sigmoid(o_ref[0].astype(F32)) * jnp.concatenate(parts, axis=-1)
        att = jnp.where(is_ctx, actx_ref[0], alat_ref[0])
        mix = jnp.concatenate([att, rec.astype(BF16)], axis=-1)
    else:
        mix = mix_ref[0]
    s = _stream_tile(h_refs, is_ctx) + mod(2) * jnp.dot(mix, wo_ref[...], preferred_element_type=F32)
    y = (_rms(s, n2_ref[...]) * (1.0 + mod(4)) + mod(3)).astype(BF16)
    d_ff = w1_ref.shape[1]
    acc = jnp.zeros(s.shape, F32)
    for c0 in range(0, d_ff, ff_chunk):
        u = jnp.maximum(jnp.dot(y, w1_ref[:, c0:c0 + ff_chunk], preferred_element_type=F32), 0.0)
        acc = acc + jnp.dot((u * u).astype(BF16), w2_ref[c0:c0 + ff_chunk, :],
                            preferred_element_type=F32)
    out = s + mod(5) * acc
    if with_final:
        out = _rms(out, fn_ref[...])
    out_ref[0] = out


def _finish(hs, mix_parts, modsx, modsc, w_out, norm2, w1, w2, *, t_len, n_rows, head_norm=None,
            final_norm=None):
    bsz, _, d = hs[0].shape
    tm = ROW_TILE
    ns = len(hs)
    with_readout = head_norm is not None
    with_final = final_norm is not None
    tok = lambda w: pl.BlockSpec((1, tm, w), lambda b, i: (b, i, 0))
    once = lambda shape: pl.BlockSpec(shape, lambda *_: (0,) * len(shape),
                                      pipeline_mode=pl.Buffered(1))
    args = list(hs) + list(mix_parts)
    specs = _stream_specs(hs, tm, t_len) + [tok(p.shape[-1]) for p in mix_parts]
    if with_readout:
        n_lat = t_len // tm
        aw = mix_parts[0].shape[-1]
        specs[ns] =pl.BlockSpec((1, tm, aw), lambda b, i: (b, jnp.minimum(i, n_lat - 1), 0))
        specs[ns + 1] =pl.BlockSpec((1, tm, aw), lambda b, i: (b, jnp.maximum(i - n_lat, 0), 0))
        args.append(head_norm.reshape(1, -1))
        specs.append(_const_spec((1, head_norm.shape[0])))
    args += [modsx, modsc, w_out, norm2.reshape(1, d), w1, w2]
    specs += [pl.BlockSpec((1, 8, d), lambda b, i: (b, 0, 0)), _const_spec((8, d)),
              once(w_out.shape), _const_spec((1, d)), once(w1.shape), once(w2.shape)]
    if with_final:
        args.append(final_norm.reshape(1, d))
        specs.append(_const_spec((1, d)))
    kern = functools.partial(_finish_kernel, n_stream=ns, tm=tm, t_len=t_len, with_readout=with_readout,
                             with_final=with_final, ff_chunk=1024)
    return pl.pallas_call(
        kern,
        grid=(bsz, n_rows // tm),
        in_specs=specs,
        out_specs=tok(d),
        out_shape=jax.ShapeDtypeStruct((bsz, n_rows, d), F32),
        compiler_params=_params(("parallel", "parallel")),
        name="finish",
    )(*args)


def _rope_tables(t_len, ctx_len, d_rot, lane0):
    n_freq = d_rot // 4
    n = 2 * n_freq
    rel = np.arange(LANES) - lane0
    first = (rel >= 0) & (rel < n)
    second = (rel >= n) & (rel < 2 * n)
    idx = np.clip(np.where(second, rel - n, rel), 0, n - 1)
    freqs = ROPE_THETA ** (-jnp.arange(n_freq, dtype=F32) / n_freq)
    t = jnp.arange(t_len + ctx_len)[:, None]
    pos = jnp.where(jnp.asarray(idx // n_freq == 0)[None, :], t // GRID_W, t % GRID_W).astype(F32)
    ang = pos * freqs[idx % n_freq][None, :]
    live = t < t_len
    cos, sin = jnp.cos(ang), jnp.sin(ang)
    c = jnp.where(live & jnp.asarray(first | second)[None, :], cos, 1.0)
    s1 = jnp.where(live & jnp.asarray(second)[None, :], sin, 0.0)
    s2 = jnp.where(live & jnp.asarray(first)[None, :], -sin, 0.0)
    return c, s1, s2


def _mla_rope_cols():
    nf = MLA_ROPE // 4
    first = [a * 2 * nf + f for a in range(2) for f in range(nf)]
    second = [a * 2 * nf + nf + f for a in range(2) for f in range(nf)]
    return np.array(first + second)


def _gqa_head_cols():
    nf = GQA_DH // 4
    first = [a * 2 * nf + f for a in range(2) for f in range(nf)]
    second = [a * 2 * nf + nf + f for a in range(2) for f in range(nf)]
    return np.array(first + second)


def kernel(x, c, ctx, c_ctx,
           l0_ada_w, l0_ada_b, l0_norm1, l0_w_in, l0_mla_q_norm, l0_mla_w_uq, l0_mla_kv_norm, l0_mla_w_ukv,
           l0_ml_conv, l0_ml_gate_b, l0_ml_head_norm, l0_w_out, l0_norm2, l0_w1, l0_w2,
           l1_ada_w, l1_ada_b, l1_norm1, l1_w_in, l1_q_norm, l1_k_norm, l1_w_out, l1_norm2, l1_w1, l1_w2,
           final_norm):
    bsz, t_len, d = x.shape
    ctx_len = ctx.shape[1]
    s_len = t_len + ctx_len

    cv = jnp.zeros((8, d), F32).at[:bsz].set(c).at[bsz].set(c_ctx)

    def mods(ada_w, ada_b):
        m = _adaln(cv, ada_w, ada_b).reshape(8, N_MOD, d)
        pad = jnp.zeros((8 - N_MOD, d), F32)
        modsx = jnp.concatenate([m[:bsz], jnp.broadcast_to(pad, (bsz,) + pad.shape)], axis=1)
        modsc = jnp.concatenate([m[bsz], pad], axis=0)
        return modsx, modsc

    modsx, modsc = mods(l0_ada_w, l0_ada_b)
    o = np.cumsum([0, MLA_Q_RANK, MLA_KV_RANK, MLA_ROPE, 2 * ML_HEADS * ML_DK, ML_HEADS * ML_DV,
                   ML_HEADS * ML_DV, N_GATES])
    w_cq, w_ckv, w_kpe, w_qk, w_v, w_o, w_g = (l0_w_in[:, o[i]:o[i + 1]] for i in range(7))
    side_pad = jnp.zeros((d, LANES - MLA_ROPE - N_GATES), F32)
    w_in0 = jnp.concatenate([w_qk, w_v, w_o, w_cq, w_ckv, w_kpe[:, _mla_rope_cols()], w_g, side_pad],
                            axis=1).astype(BF16)
    widths0 = (2 * ML_HEADS * ML_DK, ML_HEADS * ML_DV, ML_HEADS * ML_DV, MLA_Q_RANK + MLA_KV_RANK, LANES)
    qk_pre, v_ml, o_pre, lat, side = _inproj((x, ctx), l0_norm1, modsx, modsc, w_in0, widths0,
                                             (BF16, BF16, BF16, BF16, F32), t_len, s_len)

    dq = MLA_NOPE + MLA_ROPE
    head_cols = np.concatenate([np.arange(MLA_NOPE), MLA_NOPE + _mla_rope_cols()])
    wq = l0_mla_w_uq.reshape(MLA_Q_RANK, MLA_HEADS, dq)[:, :, head_cols]
    wq = jnp.pad(wq, ((0, 0), (0, 0), (0, LANES - dq))).reshape(MLA_Q_RANK, MLA_HEADS * LANES).astype(BF16)
    wkv = l0_mla_w_ukv.reshape(MLA_KV_RANK, MLA_HEADS, MLA_NOPE + MLA_V)
    wk_nope = jnp.pad(wkv[:, :, :MLA_NOPE], ((0, 0), (0, 0), (0, LANES - MLA_NOPE)))
    place = np.zeros((LANES, MLA_HEADS, LANES), np.float32)
    for hh in range(MLA_HEADS):
        place[KPE_OFF + np.arange(MLA_ROPE), hh, MLA_NOPE + np.arange(MLA_ROPE)] = 1.0
    wk = jnp.concatenate([wk_nope, jnp.asarray(place)], axis=0).reshape(
        MLA_KV_RANK + LANES, MLA_HEADS * LANES).astype(BF16)
    wv = wkv[:, :, MLA_NOPE:].reshape(MLA_KV_RANK, MLA_HEADS * MLA_V).astype(BF16)
    tabs_mla = _rope_tables(t_len, ctx_len, MLA_ROPE, MLA_NOPE)
    q_mla, k_mla, v_mla = _mla_prep(lat, side, tabs_mla, l0_mla_q_norm, l0_mla_kv_norm, wq.T, wk, wv.T)
    mla_att = functools.partial(_attention, q_mla, k_mla, v_mla, n_groups=MLA_HEADS // 2, n_heads=2,
                                k_shared=False, v_pair=True, t_len=t_len)
    a_lat, a_ctx = mla_att(latent_queries=True), mla_att(latent_queries=False)

    qk = _conv_silu(qk_pre, l0_ml_conv, t_len)
    bias_row = jnp.zeros((1, LANES), F32).at[0, GATE_OFF:GATE_OFF + N_GATES].set(l0_ml_gate_b)
    hf, hb = _mlstm(qk, v_ml, side, bias_row, t_len)
    h = _finish((x, ctx), (a_lat, a_ctx, hf, hb, o_pre), modsx, modsc, l0_w_out.astype(BF16),
                l0_norm2, l0_w1.astype(BF16), l0_w2.astype(BF16), t_len=t_len, n_rows=s_len,
                head_norm=l0_ml_head_norm)

    modsx, modsc = mods(l1_ada_w, l1_ada_b)
    hc = _gqa_head_cols()
    n_qk = GQA_HEADS + GQA_KV_HEADS
    qk_cols = (np.arange(n_qk)[:, None] * GQA_DH + hc[None, :]).reshape(-1)
    n_q = GQA_HEADS * GQA_DH
    wqv_t = jnp.concatenate([l1_w_in[:, qk_cols[:n_q]], l1_w_in[:, n_qk * GQA_DH:]], axis=1).T.astype(BF16)
    wk1 = l1_w_in[:, qk_cols[n_q:]].astype(BF16)
    tabs_gqa = _rope_tables(t_len, ctx_len, GQA_DH, 0)
    q_gqa, k_gqa, v_gqa = _gqa_inproj(h, l1_norm1, modsx, modsc, wqv_t, wk1, tabs_gqa,
                                      l1_q_norm[hc], l1_k_norm[hc], t_len)
    att = _attention(q_gqa, k_gqa, v_gqa, n_groups=GQA_KV_HEADS, n_heads=GQA_GROUP, k_shared=True,
                     v_pair=False, t_len=t_len, latent_queries=True)
    return _finish((h,), (att,), modsx, modsc, l1_w_out.astype(BF16), l1_norm2,
                   l1_w1.astype(BF16), l1_w2.astype(BF16), t_len=t_len, n_rows=t_len,
                   final_norm=final_norm)
```
